```python
import numpy as np
import jax, jax.numpy as jnp
from jax import lax

D_MODEL = 1024
BATCH = 32
SEQ = 2048
DEPTH = 2

N_A_LAYERS = max(1, DEPTH // 2)
N_B_LAYERS = DEPTH - N_A_LAYERS

D_RNN = 1536
N_RNN_BLOCKS = 16
RNN_BLOCK = D_RNN // N_RNN_BLOCKS
CONV_WIDTH = 4
RGLRU_C = 8.0

D_FF = -(-8 * D_MODEL // (3 * 256)) * 256

N_HEADS = 16
N_KV_GROUPS = 4
HEADS_PER_GROUP = N_HEADS // N_KV_GROUPS
D_QK = 96
D_V = 64
ROPE_DIMS = D_QK // 4
ROPE_THETA = 500000.0
CMP_BLOCK = 32
CMP_STRIDE = 16
CMP_HIDDEN = 256
SEL_BLOCK = 64
SEL_TOPK = 8
WINDOW = 512
Q_BLOCK = 128
N_BRANCH = 3
EPS = 1e-6

kernel_name = "hybrid_rglru_nsa_yoco"


def rmsnorm(x, g):
    xf = x.astype(jnp.float32)
    y = xf * lax.rsqrt(jnp.mean(xf * xf, axis=-1, keepdims=True) + EPS)
    return (y * g.astype(jnp.float32)).astype(x.dtype)


def apply_partial_rope(x):
    S = x.shape[1]
    half = ROPE_DIMS // 2
    inv = ROPE_THETA ** (-jnp.arange(half, dtype=jnp.float32) * 2.0 / ROPE_DIMS)
    ang = jnp.arange(S, dtype=jnp.float32)[:, None] * inv[None, :]
    bshape = (1, S) + (1,) * (x.ndim - 3) + (half,)
    cos = jnp.cos(ang).reshape(bshape)
    sin = jnp.sin(ang).reshape(bshape)
    xr = x[..., :ROPE_DIMS].astype(jnp.float32)
    x1, x2 = xr[..., :half], xr[..., half:]
    rot = jnp.concatenate([x1 * cos - x2 * sin, x2 * cos + x1 * sin], axis=-1)
    return jnp.concatenate([rot.astype(x.dtype), x[..., ROPE_DIMS:]], axis=-1)


def masked_softmax(s, mask):
    s = jnp.where(mask, s, -jnp.inf)
    m = jnp.max(s, axis=-1, keepdims=True)
    m = jnp.where(jnp.isfinite(m), m, 0.0)
    e = jnp.where(mask, jnp.exp(s - m), 0.0)
    d = jnp.sum(e, axis=-1, keepdims=True)
    return e / jnp.where(d > 0, d, 1.0)


def swiglu(h, w_gu, w_down):
    u = h @ w_gu
    g, v = u[..., :D_FF], u[..., D_FF:]
    return (jax.nn.silu(g) * v) @ w_down


def rglru_mixer(h, w_in, conv_w, conv_b, w_ra, b_ra, w_ix, b_ix, lam, w_out):
    B, S, _ = h.shape
    u = h @ w_in
    xb, yb = u[..., :D_RNN], u[..., D_RNN:]
    y = jax.nn.gelu(yb)
    xc = lax.conv_general_dilated(
        xb, conv_w[:, None, :], window_strides=(1,), padding=[(CONV_WIDTH - 1, 0)],
        dimension_numbers=('NWC', 'WIO', 'NWC'), feature_group_count=D_RNN) + conv_b
    xg = xc.reshape(B, S, N_RNN_BLOCKS, RNN_BLOCK)
    r = jax.nn.sigmoid(jnp.einsum('bsni,nij->bsnj', xg, w_ra) + b_ra).reshape(B, S, D_RNN)
    i = jax.nn.sigmoid(jnp.einsum('bsni,nij->bsnj', xg, w_ix) + b_ix).reshape(B, S, D_RNN)
    log_a = -RGLRU_C * r.astype(jnp.float32) * jax.nn.softplus(-lam.astype(jnp.float32))
    a = jnp.exp(log_a)
    mult = jnp.sqrt(-jnp.expm1(2.0 * log_a))
    bx = mult * (i * xc).astype(jnp.float32)

    def step(hprev, inp):
        a_t, b_t = inp
        hn = a_t * hprev + b_t
        return hn, hn

    _, hs = lax.scan(step, jnp.zeros((B, D_RNN), jnp.float32),
                     (a.transpose(1, 0, 2), bx.transpose(1, 0, 2)))
    rec = hs.transpose(1, 0, 2).astype(h.dtype)
    return (rec * y) @ w_out


def compress_blocks(raw, pos, w1, w2):
    B, S, G, Dh = raw.shape
    nc = (S - CMP_BLOCK) // CMP_STRIDE + 1
    idx = np.arange(nc)[:, None] * CMP_STRIDE + np.arange(CMP_BLOCK)[None, :]
    blk = raw[:, idx] + pos[None, None, :, None, :]
    blk = blk.transpose(0, 1, 3, 2, 4).reshape(B, nc, G, CMP_BLOCK * Dh)
    return jax.nn.gelu(blk @ w1) @ w2


def shared_kv(h, kv_norm_g, kv_w, cmp_pos_k, cmp_w1_k, cmp_w2_k, cmp_pos_v, cmp_w1_v, cmp_w2_v):
    B, S, _ = h.shape
    hn = rmsnorm(h, kv_norm_g)
    kv = hn @ kv_w
    gk, gv = N_KV_GROUPS * D_QK, N_KV_GROUPS * D_V
    sizes = [gk, gv, gk, gv, gk, gv]
    offs = np.concatenate([[0], np.cumsum(sizes)])
    parts = [kv[..., offs[j]:offs[j + 1]] for j in range(6)]
    k_cmp, v_cmp, k_sel, v_sel, k_win, v_win = [
        p.reshape(B, S, N_KV_GROUPS, -1) for p in parts]
    kc = compress_blocks(k_cmp, cmp_pos_k, cmp_w1_k, cmp_w2_k)
    vc = compress_blocks(v_cmp, cmp_pos_v, cmp_w1_v, cmp_w2_v)
    ks = apply_partial_rope(k_sel)
    kw = apply_partial_rope(k_win)
    pad = ((0, 0), (WINDOW, 0), (0, 0), (0, 0))
    kw_pad = jnp.pad(kw, pad)
    vw_pad = jnp.pad(v_win, pad)
    return (kc, vc, ks, v_sel, kw_pad, vw_pad)


def nsa_mixer(h, w_q, gate_bias, w_o, kc, vc, ks, vs, kw_pad, vw_pad):
    B, S, _ = h.shape
    G, HPG = N_KV_GROUPS, HEADS_PER_GROUP
    nc = kc.shape[1]
    ns = S // SEL_BLOCK
    nq = S // Q_BLOCK
    k_top = min(SEL_TOPK, ns)
    cs = np.arange(nc) * CMP_STRIDE
    ss = np.arange(ns) * SEL_BLOCK
    overlap = jnp.asarray(((cs[:, None] < ss[None, :] + SEL_BLOCK) &
                           (cs[:, None] + CMP_BLOCK > ss[None, :])).astype(np.float32))
    c_end = jnp.arange(nc) * CMP_STRIDE + CMP_BLOCK - 1
    blk_ids = jnp.arange(ns)

    u = h @ w_q
    q = (u[..., :N_HEADS * D_QK] * (D_QK ** -0.5)).reshape(B, S, G, HPG, D_QK)
    gates = jax.nn.sigmoid(u[..., N_HEADS * D_QK:] + gate_bias).reshape(B, S, G, HPG, N_BRANCH)
    q_rope = apply_partial_rope(q)

    def per_seq(args):
        qn, qr, gt, kc_b, vc_b, ks_s, vs_s, kw_b, vw_b = args
        ks_blk = ks_s.reshape(ns, SEL_BLOCK, G, D_QK).transpose(2, 0, 1, 3)
        vs_blk = vs_s.reshape(ns, SEL_BLOCK, G, D_V).transpose(2, 0, 1, 3)
        gidx = jnp.arange(G)[None, :, None]

        def per_block(bargs):
            qn_t, qr_t, g_t, qi = bargs
            s0 = qi * Q_BLOCK
            t = s0 + jnp.arange(Q_BLOCK)
            s_c = jnp.einsum('tghd,cgd->tghc', qn_t, kc_b).astype(jnp.float32)
            mask_c = (c_end[None, :] <= t[:, None])[:, None, None, :]
            p_c = masked_softmax(s_c, mask_c)
            o_c = jnp.einsum('tghc,cgv->tghv', p_c.astype(vc_b.dtype), vc_b)
            imp = jnp.einsum('tghc,cn->tgn', p_c, overlap)
            cur = (t // SEL_BLOCK)[:, None, None]
            visible = (blk_ids * SEL_BLOCK)[None, None, :] <= t[:, None, None]
            forced = (blk_ids == 0) | (blk_ids == cur) | (blk_ids == cur - 1)
            score = jnp.where(forced, jnp.inf, jnp.where(visible, imp, -jnp.inf))
            _, idx = lax.top_k(score, k_top)
            kg = ks_blk[gidx, idx]
            vg = vs_blk[gidx, idx]
            s_s = jnp.einsum('tghd,tgjld->tghjl', qr_t, kg).astype(jnp.float32)
            pos = idx[..., None] * SEL_BLOCK + jnp.arange(SEL_BLOCK)
            mask_s = (pos <= t[:, None, None, None]).reshape(Q_BLOCK, G, 1, k_top * SEL_BLOCK)
            p_s = masked_softmax(s_s.reshape(Q_BLOCK, G, HPG, k_top * SEL_BLOCK), mask_s)
            p_s = p_s.reshape(Q_BLOCK, G, HPG, k_top, SEL_BLOCK)
            o_s = jnp.einsum('tghjl,tgjlv->tghv', p_s.astype(vg.dtype), vg)
            kwin = lax.dynamic_slice_in_dim(kw_b, s0, WINDOW + Q_BLOCK, axis=0)
            vwin = lax.dynamic_slice_in_dim(vw_b, s0, WINDOW + Q_BLOCK, axis=0)
            p_pos = s0 - WINDOW + jnp.arange(WINDOW + Q_BLOCK)
            mask_w = ((p_pos[None, :] >= 0) & (p_pos[None, :] <= t[:, None]) &
                      (p_pos[None, :] > t[:, None] - WINDOW))[:, None, None, :]
            s_w = jnp.einsum('tghd,sgd->tghs', qr_t, kwin).astype(jnp.float32)
            p_w = masked_softmax(s_w, mask_w)
            o_w = jnp.einsum('tghs,sgv->tghv', p_w.astype(vwin.dtype), vwin)
            return g_t[..., 0:1] * o_c + g_t[..., 1:2] * o_s + g_t[..., 2:3] * o_w

        out = lax.map(per_block, (
            qn.reshape(nq, Q_BLOCK, G, HPG, D_QK),
            qr.reshape(nq, Q_BLOCK, G, HPG, D_QK),
            gt.reshape(nq, Q_BLOCK, G, HPG, N_BRANCH),
            jnp.arange(nq)))
        return out.reshape(S, N_HEADS * D_V)

    o = lax.map(per_seq, (q, q_rope, gates, kc, vc, ks, vs, kw_pad, vw_pad))
    return o @ w_o


def setup_inputs(seed: int = 0) -> dict:
    key = jax.random.key(seed)
    ks = jax.random.split(key, 32)
    f32 = jnp.float32
    res = (2.0 * DEPTH) ** -0.5

    def nrm(k, shape, scale):
        return jax.random.normal(k, shape, f32) * scale

    kv_cols = N_BRANCH * N_KV_GROUPS * (D_QK + D_V)
    q_cols = N_HEADS * D_QK + N_HEADS * N_BRANCH
    u = jax.random.uniform(ks[9], (N_A_LAYERS, D_RNN), f32, 0.9, 0.999)
    s = u ** (1.0 / RGLRU_C)
    a_lambda = jnp.log(s) - jnp.log1p(-s)
    return {
        'x': nrm(ks[0], (BATCH, SEQ, D_MODEL), 1.0),
        'norm_g': 1.0 + nrm(ks[1], (DEPTH, 2, D_MODEL), 0.02),
        'ffn_w_gu': nrm(ks[2], (DEPTH, D_MODEL, 2 * D_FF), D_MODEL ** -0.5),
        'ffn_w_down': nrm(ks[3], (DEPTH, D_FF, D_MODEL), res * D_FF ** -0.5),
        'a_w_in': nrm(ks[4], (N_A_LAYERS, D_MODEL, 2 * D_RNN), D_MODEL ** -0.5),
        'a_conv_w': nrm(ks[5], (N_A_LAYERS, CONV_WIDTH, D_RNN), CONV_WIDTH ** -0.5),
        'a_conv_b': nrm(ks[6], (N_A_LAYERS, D_RNN), 0.01),
        'a_w_ra': nrm(ks[7], (N_A_LAYERS, N_RNN_BLOCKS, RNN_BLOCK, RNN_BLOCK), RNN_BLOCK ** -0.5),
        'a_b_ra': nrm(ks[8], (N_A_LAYERS, N_RNN_BLOCKS, RNN_BLOCK), 0.01),
        'a_w_ix': nrm(ks[10], (N_A_LAYERS, N_RNN_BLOCKS, RNN_BLOCK, RNN_BLOCK), RNN_BLOCK ** -0.5),
        'a_b_ix': nrm(ks[11], (N_A_LAYERS, N_RNN_BLOCKS, RNN_BLOCK), 0.01),
        'a_lambda': a_lambda,
        'a_w_out': nrm(ks[12], (N_A_LAYERS, D_RNN, D_MODEL), res * D_RNN ** -0.5),
        'kv_norm_g': 1.0 + nrm(ks[13], (D_MODEL,), 0.02),
        'kv_w': nrm(ks[14], (D_MODEL, kv_cols), D_MODEL ** -0.5),
        'cmp_pos_k': nrm(ks[15], (CMP_BLOCK, D_QK), 0.1),
        'cmp_w1_k': nrm(ks[16], (CMP_BLOCK * D_QK, CMP_HIDDEN), (CMP_BLOCK * D_QK) ** -0.5),
        'cmp_w2_k': nrm(ks[17], (CMP_HIDDEN, D_QK), CMP_HIDDEN ** -0.5),
        'cmp_pos_v': nrm(ks[18], (CMP_BLOCK, D_V), 0.1),
        'cmp_w1_v': nrm(ks[19], (CMP_BLOCK * D_V, CMP_HIDDEN), (CMP_BLOCK * D_V) ** -0.5),
        'cmp_w2_v': nrm(ks[20], (CMP_HIDDEN, D_V), CMP_HIDDEN ** -0.5),
        'b_w_q': nrm(ks[21], (N_B_LAYERS, D_MODEL, q_cols), D_MODEL ** -0.5),
        'b_gate_bias': nrm(ks[22], (N_B_LAYERS, N_HEADS * N_BRANCH), 0.01),
        'b_w_o': nrm(ks[23], (N_B_LAYERS, N_HEADS * D_V, D_MODEL), res * (N_HEADS * D_V) ** -0.5),
        'final_g': 1.0 + nrm(ks[24], (D_MODEL,), 0.02),
    }


def reference(x, norm_g, ffn_w_gu, ffn_w_down, a_w_in, a_conv_w, a_conv_b, a_w_ra, a_b_ra,
              a_w_ix, a_b_ix, a_lambda, a_w_out, kv_norm_g, kv_w, cmp_pos_k, cmp_w1_k, cmp_w2_k,
              cmp_pos_v, cmp_w1_v, cmp_w2_v, b_w_q, b_gate_bias, b_w_o, final_g):
    h = x
    shared = None
    for layer in range(DEPTH):
        hn = rmsnorm(h, norm_g[layer, 0])
        if layer < N_A_LAYERS:
            h = h + rglru_mixer(hn, a_w_in[layer], a_conv_w[layer], a_conv_b[layer],
                                a_w_ra[layer], a_b_ra[layer], a_w_ix[layer], a_b_ix[layer],
                                a_lambda[layer], a_w_out[layer])
        else:
            j = layer - N_A_LAYERS
            kc, vc, ks_, vs_, kw_pad, vw_pad = shared
            h = h + nsa_mixer(hn, b_w_q[j], b_gate_bias[j], b_w_o[j],
                              kc, vc, ks_, vs_, kw_pad, vw_pad)
        h = h + swiglu(rmsnorm(h, norm_g[layer, 1]), ffn_w_gu[layer], ffn_w_down[layer])
        if layer == N_A_LAYERS - 1:
            shared = shared_kv(h, kv_norm_g, kv_w, cmp_pos_k, cmp_w1_k, cmp_w2_k,
                               cmp_pos_v, cmp_w1_v, cmp_w2_v)
    return rmsnorm(h, final_g)
```

```python
import functools
import math

import numpy as np
import jax
import jax.numpy as jnp
from jax import lax
from jax.experimental import pallas as pl
from jax.experimental.pallas import tpu as pltpu

CONV_WIDTH = 4
RGLRU_C = 8.0
N_HEADS = 16
N_KV_GROUPS = 4
HPG = N_HEADS // N_KV_GROUPS
D_QK = 96
D_V = 64
ROPE_DIMS = D_QK // 4
ROPE_HALF = ROPE_DIMS // 2
ROPE_THETA = 500000.0
CMP_BLOCK = 32
CMP_STRIDE = 16
SEL_BLOCK = 64
SEL_TOPK = 8
WINDOW = 512
Q_BLOCK = 128
N_BRANCH = 3
EPS = 1e-6

LANES = 128
SLOT = 128
AUG = SLOT - D_QK
VMEM_LIMIT = 56 * 1024 * 1024
NEG = -1e30
SEL_BIAS = -1e9

F32 = jnp.float32
BF16 = jnp.bfloat16


def _cparams(sem):
    return pltpu.CompilerParams(dimension_semantics=sem, vmem_limit_bytes=VMEM_LIMIT)


def _const_spec(shape):
    nd = len(shape)
    return pl.BlockSpec(shape, lambda *_: (0,) * nd, pipeline_mode=pl.Buffered(1))


def _rms(x, g):
    return x * lax.rsqrt(jnp.mean(x * x, axis=-1, keepdims=True) + EPS) * g


def _dot(a, b):
    return jnp.dot(a, b, preferred_element_type=F32)


def _dot_nt(a, b):
    return lax.dot_general(a, b, (((1,), (1,)), ((), ())), preferred_element_type=F32)


def _rglru_kernel(x_ref, g_ref, win_ref, cw_ref, cb_ref, wg_ref, bra_ref, bix_ref, lam_ref,
                  wout_ref, o_ref, xbe_ref, h_ref, hn_ref, xin_ref, ry_ref, *,
                  ts, nb, d, cw, nch, in_bm):
    rows = ts * nb
    halo = (CONV_WIDTH - 1) * nb

    @pl.when(pl.program_id(0) == 0)
    def _():
        xbe_ref[:, 0:halo, :] = jnp.zeros((nch, halo, cw), F32)
        h_ref[...] = jnp.zeros_like(h_ref)

    g = g_ref[...]
    if in_bm:
        for t in range(ts):
            xt = x_ref[:, t * d:(t + 1) * d]
            xin_ref[t * nb:(t + 1) * nb, :] = xt
            hn_ref[t * nb:(t + 1) * nb, :] = _rms(xt, g).astype(BF16)
    else:
        xt = x_ref[...]
        xin_ref[...] = xt
        hn_ref[...] = _rms(xt, g).astype(BF16)

    hn = hn_ref[...]
    for c in range(nch):
        u = _dot(hn, win_ref[c])
        xbe_ref[c, halo:halo + rows, :] = u[:, :cw]
        y = jax.nn.gelu(u[:, cw:])
        xc = cb_ref[c]
        for j in range(CONV_WIDTH):
            xc = xc + cw_ref[c, j:j + 1, :] * xbe_ref[c, j * nb:j * nb + rows, :]
        xbe_ref[c, 0:halo, :] = xbe_ref[c, rows:rows + halo, :]
        gt = _dot(xc.astype(BF16), wg_ref[c])
        r = jax.nn.sigmoid(gt[:, :cw] + bra_ref[c])
        ig = jax.nn.sigmoid(gt[:, cw:] + bix_ref[c])
        log_a = (-RGLRU_C) * r * jax.nn.softplus(-lam_ref[c])
        a = jnp.exp(log_a)
        bx = jnp.sqrt(jnp.tanh(-log_a) * (1.0 + a * a)) * (ig * xc)
        h = h_ref[c]
        for t in range(ts):
            sl = slice(t * nb, (t + 1) * nb)
            h = a[sl] * h + bx[sl]
            ry_ref[sl, c * cw:(c + 1) * cw] = (h * y[sl]).astype(BF16)
        h_ref[c] = h
    o_ref[...] = _dot(ry_ref[...], wout_ref[...]) + xin_ref[...]


def _rglru_mixer(x, g, w_in, conv_w, conv_b, w_ra, b_ra, w_ix, b_ix, lam, w_out, *, nb, s, in_bm):
    d = w_in.shape[0]
    r = w_in.shape[1] // 2
    nblk, rb = w_ra.shape[0], w_ra.shape[1]
    per = LANES // math.gcd(rb, LANES)
    cw = per * rb
    nch = nblk // per
    assert nch * per == nblk and nch * cw == r
    ts = 16
    assert s % ts == 0 and nb % 16 == 0
    rows = ts * nb

    def chunks(v):
        return v.reshape(v.shape[0], nch, cw).transpose(1, 0, 2)

    def blockdiag(w):
        w = w.reshape(nch, per, rb, rb)
        return jnp.einsum('cpij,pq->cpiqj', w, jnp.eye(per, dtype=w.dtype)).reshape(nch, cw, cw)

    win = jnp.concatenate([chunks(w_in[:, :r]), chunks(w_in[:, r:])], axis=-1).astype(BF16)
    wg = jnp.concatenate([blockdiag(w_ra), blockdiag(w_ix)], axis=-1).astype(BF16)
    args = (x, g.reshape(1, d), win, chunks(conv_w), chunks(conv_b.reshape(1, r)), wg,
            chunks(b_ra.reshape(1, r)), chunks(b_ix.reshape(1, r)), chunks(lam.reshape(1, r)),
            w_out.astype(BF16))
    x_spec = (pl.BlockSpec((nb, ts * d), lambda i: (0, i)) if in_bm
              else pl.BlockSpec((rows, d), lambda i: (i, 0)))
    in_specs = [x_spec] + [_const_spec(a.shape) for a in args[1:]]
    kern = functools.partial(_rglru_kernel, ts=ts, nb=nb, d=d, cw=cw, nch=nch, in_bm=in_bm)
    return pl.pallas_call(
        kern,
        grid=(s // ts,),
        in_specs=in_specs,
        out_specs=pl.BlockSpec((rows, d), lambda i: (i, 0)),
        out_shape=jax.ShapeDtypeStruct((s * nb, d), F32),
        scratch_shapes=[
            pltpu.VMEM((nch, rows + (CONV_WIDTH - 1) * nb, cw), F32),
            pltpu.VMEM((nch, nb, cw), F32),
            pltpu.VMEM((rows, d), BF16),
            pltpu.VMEM((rows, d), F32),
            pltpu.VMEM((rows, r), BF16),
        ],
        compiler_params=_cparams(("arbitrary",)),
        name="rglru_mixer",
    )(*args)


def _ffn_kernel(*refs, f, fc, pre_attn, final_norm, out_bm, nb, d):
    it = iter(refs)
    h_ref = next(it)
    if pre_attn:
        attn_ref, wo_ref = next(it), next(it)
    g_ref, wgu_ref, wdn_ref = next(it), next(it), next(it)
    if final_norm:
        gf_ref = next(it)
    o_ref = next(it)

    h = h_ref[...]
    if pre_attn:
        h = h + _dot(attn_ref[...], wo_ref[...])
    hn = _rms(h, g_ref[...]).astype(BF16)
    acc = h
    for c in range(f // fc):
        gate = _dot(hn, wgu_ref[:, c * fc:(c + 1) * fc])
        val = _dot(hn, wgu_ref[:, f + c * fc:f + (c + 1) * fc])
        act = (jax.nn.silu(gate) * val).astype(BF16)
        acc = acc + _dot(act, wdn_ref[c * fc:(c + 1) * fc, :])
    if final_norm:
        acc = _rms(acc, gf_ref[...])
    if out_bm:
        for t in range(acc.shape[0] // nb):
            o_ref[:, t * d:(t + 1) * d] = acc[t * nb:(t + 1) * nb, :]
    else:
        o_ref[...] = acc


def _ffn(h, g, w_gu, w_down, *, attn=None, w_o=None, final_g=None, out_bm_nb=None):
    n, d = h.shape
    f = w_down.shape[0]
    tr = 512
    fc = f // 2 if (f // 2) % LANES == 0 else f
    assert n % tr == 0 and f % fc == 0
    pre_attn = attn is not None
    final_norm = final_g is not None
    out_bm = out_bm_nb is not None
    nb = out_bm_nb if out_bm else 0
    row_spec = lambda w: pl.BlockSpec((tr, w), lambda i: (i, 0))
    args, specs = [h], [row_spec(d)]
    if pre_attn:
        args += [attn, w_o.astype(BF16)]
        specs += [row_spec(attn.shape[1]), _const_spec(w_o.shape)]
    args += [g.reshape(1, d), w_gu.astype(BF16), w_down.astype(BF16)]
    specs += [_const_spec((1, d)), _const_spec(w_gu.shape), _const_spec(w_down.shape)]
    if final_norm:
        args.append(final_g.reshape(1, d))
        specs.append(_const_spec((1, d)))
    if out_bm:
        assert tr % nb == 0
        ts = tr // nb
        out_shape = jax.ShapeDtypeStruct((nb, (n // nb) * d), F32)
        out_spec = pl.BlockSpec((nb, ts * d), lambda i: (0, i))
    else:
        out_shape = jax.ShapeDtypeStruct((n, d), F32)
        out_spec = row_spec(d)
    kern = functools.partial(_ffn_kernel, f=f, fc=fc, pre_attn=pre_attn, final_norm=final_norm,
                             out_bm=out_bm, nb=nb, d=d)
    return pl.pallas_call(
        kern, grid=(n // tr,), in_specs=specs, out_specs=out_spec, out_shape=out_shape,
        compiler_params=_cparams(("parallel",)), name="ffn",
    )(*args)


def _rope_slot(x, cos, sin_lo, sin_hi):
    return (x * cos + pltpu.roll(x, SLOT - ROPE_HALF, 1) * sin_lo
            + pltpu.roll(x, ROPE_HALF, 1) * sin_hi)


def _kv_proj_kernel(h_ref, g_ref, w_ref, cos_ref, slo_ref, shi_ref,
                    kcmp_ref, vcmp_ref, ks_ref, vs_ref, kw_ref, vw_ref, *, tr, s):
    g_, dk, dv = N_KV_GROUPS, D_QK, D_V
    hn = _rms(h_ref[...], g_ref[...]).astype(BF16)
    cos, slo, shi = cos_ref[...], slo_ref[...], shi_ref[...]
    pos0 = (pl.program_id(0) % (s // tr)) * tr
    pos = pos0 + lax.broadcasted_iota(jnp.int32, (tr, SLOT), 0)
    lane = lax.broadcasted_iota(jnp.int32, (tr, SLOT), 1)
    onehot = (lane == pos // SEL_BLOCK).astype(F32)
    ones = jnp.ones((tr, SLOT), BF16)
    off = 0
    kcmp_ref[...] = _dot(hn, w_ref[:, off:off + g_ * dk]); off += g_ * dk
    vcmp_ref[...] = _dot(hn, w_ref[:, off:off + g_ * dv]); off += g_ * dv
    for k_ref, v_ref, aug in ((ks_ref, vs_ref, True), (kw_ref, vw_ref, False)):
        for gi in range(g_):
            k = _rope_slot(_dot(hn, w_ref[:, off:off + SLOT]), cos, slo, shi); off += SLOT
            if aug:
                k = k + onehot
            k_ref[:, gi * SLOT:(gi + 1) * SLOT] = k.astype(BF16)
        for gi in range(g_):
            v2 = _dot(hn, w_ref[:, off:off + SLOT]); off += SLOT
            v_ref[:, 2 * gi * SLOT:(2 * gi + 1) * SLOT] = v2.astype(BF16)
            v_ref[:, (2 * gi + 1) * SLOT:(2 * gi + 2) * SLOT] = ones


def _q_proj_kernel(h_ref, g_ref, w_ref, gb_ref, cos_ref, slo_ref, shi_ref, qn_ref, qr_ref, gt_ref):
    hn = _rms(h_ref[...], g_ref[...]).astype(BF16)
    cos, slo, shi = cos_ref[...], slo_ref[...], shi_ref[...]
    scale = D_QK ** -0.5
    for hd in range(N_HEADS):
        q = _dot(hn, w_ref[:, hd * SLOT:(hd + 1) * SLOT]) * scale
        qn_ref[:, hd * SLOT:(hd + 1) * SLOT] = q.astype(BF16)
        qr_ref[:, hd * SLOT:(hd + 1) * SLOT] = _rope_slot(q, cos, slo, shi).astype(BF16)
    gt_ref[...] = jax.nn.sigmoid(_dot(hn, w_ref[:, N_HEADS * SLOT:]) + gb_ref[...])


def _pad_slots(w, n, width):
    d = w.shape[0]
    w = w.reshape(d, n, width)
    return jnp.pad(w, ((0, 0), (0, 0), (SLOT - width, 0))).reshape(d, n * SLOT)


def _rope_tables(s):
    inv = ROPE_THETA ** (-jnp.arange(ROPE_HALF, dtype=F32) * 2.0 / ROPE_DIMS)
    ang = jnp.arange(s, dtype=F32)[:, None] * inv[None, :]
    cos, sin = jnp.cos(ang), jnp.sin(ang)
    z = lambda n: jnp.zeros((s, n), F32)
    o = lambda n: jnp.ones((s, n), F32)
    rest = SLOT - AUG - ROPE_DIMS
    cos_t = jnp.concatenate([o(AUG), cos, cos, o(rest)], axis=1)
    sin_lo = jnp.concatenate([z(AUG), -sin, z(ROPE_HALF), z(rest)], axis=1)
    sin_hi = jnp.concatenate([z(AUG), z(ROPE_HALF), sin, z(rest)], axis=1)
    return cos_t, sin_lo, sin_hi


def _kv_proj(h, g, kv_w, tables, *, s):
    n, d = h.shape
    g_, dk, dv = N_KV_GROUPS, D_QK, D_V
    tr = 512
    assert n % tr == 0 and s % tr == 0
    sizes = [g_ * dk, g_ * dv] * 3
    offs = np.concatenate([[0], np.cumsum(sizes)])
    kc, vc, ksl, vsl, kwn, vwn = [kv_w[:, offs[j]:offs[j + 1]] for j in range(6)]
    dup = lambda v: jnp.tile(v.reshape(d, g_, 1, dv), (1, 1, SLOT // dv, 1)).reshape(d, g_ * SLOT)
    w = jnp.concatenate([kc, vc, _pad_slots(ksl, g_, dk), dup(vsl), _pad_slots(kwn, g_, dk), dup(vwn)],
                        axis=1).astype(BF16)
    row = lambda wd, dt: (pl.BlockSpec((tr, wd), lambda i: (i, 0)), jax.ShapeDtypeStruct((n, wd), dt))
    outs = [row(g_ * dk, F32), row(g_ * dv, F32), row(g_ * SLOT, BF16), row(2 * g_ * SLOT, BF16),
            row(g_ * SLOT, BF16), row(2 * g_ * SLOT, BF16)]
    tab_spec = pl.BlockSpec((tr, SLOT), lambda i: (i % (s // tr), 0))
    return pl.pallas_call(
        functools.partial(_kv_proj_kernel, tr=tr, s=s),
        grid=(n // tr,),
        in_specs=[pl.BlockSpec((tr, d), lambda i: (i, 0)), _const_spec((1, d)), _const_spec(w.shape),
                  tab_spec, tab_spec, tab_spec],
        out_specs=[o[0] for o in outs], out_shape=[o[1] for o in outs],
        compiler_params=_cparams(("parallel",)), name="kv_proj",
    )(h, g.reshape(1, d), w, *tables)


def _q_proj(h, g, w_q, gate_bias, tables, *, s):
    n, d = h.shape
    tr = 512
    nq = N_HEADS * D_QK
    ngate = N_HEADS * N_BRANCH
    assert ngate <= LANES
    w = jnp.concatenate([_pad_slots(w_q[:, :nq], N_HEADS, D_QK),
                         jnp.pad(w_q[:, nq:], ((0, 0), (0, LANES - ngate)))], axis=1).astype(BF16)
    gb = jnp.pad(gate_bias.reshape(1, ngate), ((0, 0), (0, LANES - ngate)))
    tab_spec = pl.BlockSpec((tr, SLOT), lambda i: (i % (s // tr), 0))
    row = lambda wd, dt: (pl.BlockSpec((tr, wd), lambda i: (i, 0)), jax.ShapeDtypeStruct((n, wd), dt))
    outs = [row(N_HEADS * SLOT, BF16), row(N_HEADS * SLOT, BF16), row(LANES, F32)]
    return pl.pallas_call(
        _q_proj_kernel,
        grid=(n // tr,),
        in_specs=[pl.BlockSpec((tr, d), lambda i: (i, 0)), _const_spec((1, d)), _const_spec(w.shape),
                  _const_spec((1, LANES)), tab_spec, tab_spec, tab_spec],
        out_specs=[o[0] for o in outs], out_shape=[o[1] for o in outs],
        compiler_params=_cparams(("parallel",)), name="q_proj",
    )(h, g.reshape(1, d), w, gb, *tables)


def _compress_kernel(x_ref, plo_ref, phi_ref, w1lo_ref, w1hi_ref, w2_ref, o_ref):
    x = x_ref[...]
    lo = _dot((x + plo_ref[...]).astype(BF16), w1lo_ref[...])
    hi = _dot((x + phi_ref[...]).astype(BF16), w1hi_ref[...])
    pre = lo + pltpu.roll(hi, hi.shape[0] - 1, 0)
    o_ref[...] = _dot(jax.nn.gelu(pre).astype(BF16), w2_ref[...]).astype(BF16)


def _compress(raw, pos, w1, w2_slot, *, nb, s):
    g_ = N_KV_GROUPS
    dh = raw.shape[1] // g_
    nchunk = s // CMP_STRIDE
    half = CMP_STRIDE * dh
    assert CMP_BLOCK == 2 * CMP_STRIDE
    x = raw.reshape(nb, nchunk, CMP_STRIDE, g_, dh).transpose(0, 3, 1, 2, 4).reshape(nb * g_ * nchunk, half)
    n = x.shape[0]
    tr = 4 * nchunk
    assert n % tr == 0
    hid = w1.shape[1]
    p = pos.reshape(1, 2 * half)
    return pl.pallas_call(
        _compress_kernel,
        grid=(n // tr,),
        in_specs=[pl.BlockSpec((tr, half), lambda i: (i, 0)), _const_spec((1, half)), _const_spec((1, half)),
                  _const_spec((half, hid)), _const_spec((half, hid)), _const_spec((hid, SLOT))],
        out_specs=pl.BlockSpec((tr, SLOT), lambda i: (i, 0)),
        out_shape=jax.ShapeDtypeStruct((n, SLOT), BF16),
        compiler_params=_cparams(("parallel",)), name="compress",
    )(x, p[:, :half], p[:, half:], w1[:half].astype(BF16), w1[half:].astype(BF16), w2_slot.astype(BF16))


def _split3(x):
    hi = x.astype(BF16)
    r1 = x - hi.astype(F32)
    mid = r1.astype(BF16)
    lo = (r1 - mid.astype(F32)).astype(BF16)
    return hi, mid, lo


def _nsa_kernel(qn_ref, qr_ref, gt_ref, kc_ref, vc_ref, ks_ref, vs_ref, kw_ref, vw_ref, ov_ref, ex_ref,
                o_ref, s_ref, *, s, nc, ns, ktop):
    tq = Q_BLOCK
    rows = HPG * tq
    kch = 2 * tq
    wlen = WINDOW + tq
    qi = pl.program_id(1)
    s0 = qi * tq
    wstart = pl.multiple_of(jnp.maximum(s0 - WINDOW, 0), tq)

    ghi, gmid, _ = _split3(gt_ref[...])
    gx = _dot(ghi, ex_ref[...]) + _dot(gmid, ex_ref[...])

    t_row = s0 + (lax.broadcasted_iota(jnp.int32, (rows, 1), 0) & (tq - 1))
    lane = lax.broadcasted_iota(jnp.int32, (tq, SLOT), 1)
    hw = N_HEADS * D_V

    for g in range(N_KV_GROUPS):
        heads = [g * HPG + i for i in range(HPG)]
        qn = jnp.concatenate([qn_ref[:, hd * SLOT:(hd + 1) * SLOT] for hd in heads], axis=0)

        st = _dot_nt(kc_ref[g], qn)
        c_idx = lax.broadcasted_iota(jnp.int32, st.shape, 0)
        t_col = s0 + (lax.broadcasted_iota(jnp.int32, st.shape, 1) & (tq - 1))
        valid = (c_idx * CMP_STRIDE + (CMP_BLOCK - 1) <= t_col) & (c_idx < nc)
        st = jnp.where(valid, st, NEG)
        e = jnp.where(valid, jnp.exp(st - jnp.max(st, axis=0, keepdims=True)), 0.0)
        den = jnp.sum(e, axis=0, keepdims=True)
        pt = e / jnp.where(den > 0, den, 1.0)
        o_c = _dot(pt.T.astype(BF16), vc_ref[g])

        psum = pt[:, 0:tq]
        for i in range(1, HPG):
            psum = psum + pt[:, i * tq:(i + 1) * tq]
        imp = sum(_dot(ov_ref[...], part) for part in _split3(psum))[0:ns, :]
        n_i = lax.broadcasted_iota(jnp.int32, (ns, tq), 0)
        tt = s0 + lax.broadcasted_iota(jnp.int32, (ns, tq), 1)
        cur = tt // SEL_BLOCK
        forced = (n_i == 0) | (n_i == cur) | (n_i == cur - 1)
        score = jnp.where(forced, jnp.inf, jnp.where(n_i * SEL_BLOCK <= tt, imp, -jnp.inf))
        sel = jnp.zeros((ns, tq), jnp.bool_)
        for _ in range(ktop):
            mx = jnp.max(score, axis=0, keepdims=True)
            idx = jnp.min(jnp.where(score == mx, n_i, ns), axis=0, keepdims=True)
            hit = n_i == idx
            sel = sel | hit
            score = jnp.where(hit, -jnp.inf, score)
        bias_t = jnp.where(sel, 0.0, SEL_BIAS)
        bias_t = jnp.concatenate([bias_t, jnp.zeros((SLOT - ns, tq), F32)], axis=0)
        bias = bias_t.T.astype(BF16)

        qr_h = [qr_ref[:, hd * SLOT:(hd + 1) * SLOT] for hd in heads]
        qr = jnp.concatenate(qr_h, axis=0)
        qa = jnp.concatenate([q + bias for q in qr_h], axis=0)

        nchunk = (qi + 2) // 2

        def score_chunk(j, mx):
            k0 = pl.multiple_of(j * kch, kch)
            sc = _dot_nt(qa, ks_ref[pl.ds(k0, kch), g * SLOT:(g + 1) * SLOT])
            kpos = k0 + lax.broadcasted_iota(jnp.int32, (rows, kch), 1)
            sc = jnp.where(kpos <= t_row, sc, NEG)
            s_ref[j] = sc
            return jnp.maximum(mx, jnp.maximum(sc[:, :SLOT], sc[:, SLOT:]))

        mx = lax.fori_loop(0, nchunk, score_chunk, jnp.full((rows, SLOT), NEG, F32))
        m = jnp.max(mx, axis=1, keepdims=True)

        def pv_chunk(j, acc):
            k0 = pl.multiple_of(j * kch, kch)
            p = jnp.exp(s_ref[j] - m).astype(BF16)
            return acc + _dot(p, vs_ref[pl.ds(k0, kch), 2 * g * SLOT:(2 * g + 2) * SLOT])

        acc = lax.fori_loop(0, nchunk, pv_chunk, jnp.zeros((rows, 2 * SLOT), F32))
        o_s = acc[:, :SLOT] / acc[:, SLOT:]

        sw = _dot_nt(qr, kw_ref[pl.ds(wstart, wlen), g * SLOT:(g + 1) * SLOT])
        kpos = wstart + lax.broadcasted_iota(jnp.int32, (rows, wlen), 1)
        sw = jnp.where((kpos <= t_row) & (kpos > t_row - WINDOW), sw, NEG)
        pw = jnp.exp(sw - jnp.max(sw, axis=1, keepdims=True)).astype(BF16)
        accw = _dot(pw, vw_ref[pl.ds(wstart, wlen), 2 * g * SLOT:(2 * g + 2) * SLOT])
        o_w = accw[:, :SLOT] / accw[:, SLOT:]

        for hp in range(HPG // 2):
            col = (g * HPG + 2 * hp) * D_V
            ev, od = slice(2 * hp * tq, (2 * hp + 1) * tq), slice((2 * hp + 1) * tq, (2 * hp + 2) * tq)
            tile = jnp.zeros((tq, SLOT), F32)
            for br, o_b in enumerate((o_c, o_s, o_w)):
                pair = jnp.where(lane < D_V, o_b[ev], o_b[od])
                tile = tile + gx[:, br * hw + col:br * hw + col + SLOT] * pair
            o_ref[:, col:col + SLOT] = tile.astype(BF16)


def _nsa_attn(qn, qr, gates, kc, vc, ks, vs, kw, vw, *, nb, s):
    tq = Q_BLOCK
    nq = s // tq
    nc = (s - CMP_BLOCK) // CMP_STRIDE + 1
    ncp = s // CMP_STRIDE
    ns = s // SEL_BLOCK
    ktop = min(SEL_TOPK, ns)
    g_ = N_KV_GROUPS
    assert ncp % 8 == 0 and ns <= AUG and s >= WINDOW + tq and s % (2 * tq) == 0 and 2 * D_V == SLOT
    hw = N_HEADS * D_V
    cs = np.arange(ncp) * CMP_STRIDE
    ss = np.arange(SLOT) * SEL_BLOCK
    ov = ((cs[None, :] < ss[:, None] + SEL_BLOCK) & (cs[None, :] + CMP_BLOCK > ss[:, None])
          & (np.arange(ncp)[None, :] < nc) & (np.arange(SLOT)[:, None] < ns))
    ex = np.zeros((LANES, N_BRANCH * hw), np.float32)
    for hd in range(N_HEADS):
        for br in range(N_BRANCH):
            ex[hd * N_BRANCH + br, br * hw + hd * D_V:br * hw + (hd + 1) * D_V] = 1.0
    qspec = lambda wd: pl.BlockSpec((tq, wd), lambda b, q: (b * nq + q, 0))
    seq = lambda wd: pl.BlockSpec((s, wd), lambda b, q: (b, 0))
    cmp_spec = pl.BlockSpec((None, g_, ncp, SLOT), lambda b, q: (b, 0, 0, 0))
    return pl.pallas_call(
        functools.partial(_nsa_kernel, s=s, nc=nc, ns=ns, ktop=ktop),
        grid=(nb, nq),
        in_specs=[qspec(N_HEADS * SLOT), qspec(N_HEADS * SLOT), qspec(LANES), cmp_spec, cmp_spec,
                  seq(g_ * SLOT), seq(2 * g_ * SLOT), seq(g_ * SLOT), seq(2 * g_ * SLOT),
                  _const_spec(ov.shape), _const_spec(ex.shape)],
        out_specs=qspec(hw),
        out_shape=jax.ShapeDtypeStruct((nb * s, hw), BF16),
        scratch_shapes=[pltpu.VMEM((s // (2 * tq), HPG * tq, 2 * tq), F32)],
        compiler_params=_cparams(("parallel", "arbitrary")), name="nsa_attn",
    )(qn, qr, gates, kc.reshape(nb, g_, ncp, SLOT), vc.reshape(nb, g_, ncp, SLOT), ks, vs, kw, vw,
      jnp.asarray(ov, BF16), jnp.asarray(ex, BF16))


def kernel(x, norm_g, ffn_w_gu, ffn_w_down, a_w_in, a_conv_w, a_conv_b, a_w_ra, a_b_ra, a_w_ix, a_b_ix,
           a_lambda, a_w_out, kv_norm_g, kv_w, cmp_pos_k, cmp_w1_k, cmp_w2_k, cmp_pos_v, cmp_w1_v,
           cmp_w2_v, b_w_q, b_gate_bias, b_w_o, final_g):
    nb, s, d = x.shape
    n_a, n_b = a_w_in.shape[0], b_w_q.shape[0]
    depth = n_a + n_b
    assert n_a >= 1 and norm_g.shape[0] == depth

    h = x.reshape(nb, s * d)
    for layer in range(n_a):
        hm = _rglru_mixer(h, norm_g[layer, 0], a_w_in[layer], a_conv_w[layer], a_conv_b[layer],
                          a_w_ra[layer], a_b_ra[layer], a_w_ix[layer], a_b_ix[layer], a_lambda[layer],
                          a_w_out[layer], nb=nb, s=s, in_bm=(layer == 0))
        last = layer == n_a - 1
        h = _ffn(hm, norm_g[layer, 1], ffn_w_gu[layer], ffn_w_down[layer],
                 final_g=final_g if layer == depth - 1 else None,
                 out_bm_nb=nb if last else None)
    h = h.reshape(nb * s, d)
    if n_b == 0:
        return h.reshape(nb, s, d)

    tables = _rope_tables(s)
    kcmp, vcmp, ks, vs, kw, vw = _kv_proj(h, kv_norm_g, kv_w, tables, s=s)
    w2k = jnp.pad(cmp_w2_k, ((0, 0), (AUG, 0)))
    w2v = jnp.tile(cmp_w2_v, (1, SLOT // D_V))
    kc = _compress(kcmp, cmp_pos_k, cmp_w1_k, w2k, nb=nb, s=s)
    vc = _compress(vcmp, cmp_pos_v, cmp_w1_v, w2v, nb=nb, s=s)

    for j in range(n_b):
        layer = n_a + j
        qn, qr, gates = _q_proj(h, norm_g[layer, 0], b_w_q[j], b_gate_bias[j], tables, s=s)
        attn = _nsa_attn(qn, qr, gates, kc, vc, ks, vs, kw, vw, nb=nb, s=s)
        h = _ffn(h, norm_g[layer, 1], ffn_w_gu[layer], ffn_w_down[layer], attn=attn, w_o=b_w_o[j],
                 final_g=final_g if layer == depth - 1 else None)
    return h.reshape(nb, s, d)
```

```python
import functools
import math

import numpy as np
import jax
import jax.numpy as jnp
from jax import lax
from jax.experimental import pallas as pl
from jax.experimental.pallas import tpu as pltpu

CONV_WIDTH = 4
RGLRU_C = 8.0
N_HEADS = 16
N_KV_GROUPS = 4
HPG = N_HEADS // N_KV_GROUPS
D_QK = 96
D_V = 64
ROPE_DIMS = D_QK // 4
ROPE_HALF = ROPE_DIMS // 2
ROPE_THETA = 500000.0
CMP_BLOCK = 32
CMP_STRIDE = 16
SEL_BLOCK = 64
SEL_TOPK = 8
WINDOW = 512
Q_BLOCK = 128
N_BRANCH = 3
EPS = 1e-6

LANES = 128
SLOT = 128
AUG = SLOT - D_QK
VMEM_LIMIT = 56 * 1024 * 1024
NEG = -1e30
SEL_BIAS = -1e9

F32 = jnp.float32
BF16 = jnp.bfloat16


def _cparams(sem):
    return pltpu.CompilerParams(dimension_semantics=sem, vmem_limit_bytes=VMEM_LIMIT)


def _const_spec(shape):
    nd = len(shape)
    return pl.BlockSpec(shape, lambda *_: (0,) * nd, pipeline_mode=pl.Buffered(1))


def _rms(x, g):
    return x * lax.rsqrt(jnp.mean(x * x, axis=-1, keepdims=True) + EPS) * g


def _dot(a, b):
    return jnp.dot(a, b, preferred_element_type=F32)


def _dot_nt(a, b):
    return lax.dot_general(a, b, (((1,), (1,)), ((), ())), preferred_element_type=F32)


def _rglru_kernel(x_ref, g_ref, win_ref, cw_ref, cb_ref, wg_ref, bra_ref, bix_ref, lam_ref,
                  wout_ref, o_ref, xbe_ref, h_ref, hn_ref, xin_ref, ry_ref, *,
                  ts, nb, cw, nch, in_bm):
    rows = ts * nb
    halo = (CONV_WIDTH - 1) * nb

    @pl.when(pl.program_id(0) == 0)
    def _():
        xbe_ref[:, 0:halo, :] = jnp.zeros((nch, halo, cw), F32)
        h_ref[...] = jnp.zeros_like(h_ref)

    g = g_ref[...]
    if in_bm:
        for t in range(ts):
            xt = x_ref[:, t, :]
            xin_ref[t * nb:(t + 1) * nb, :] = xt
            hn_ref[t * nb:(t + 1) * nb, :] = _rms(xt, g).astype(BF16)
    else:
        xt = x_ref[...]
        xin_ref[...] = xt
        hn_ref[...] = _rms(xt, g).astype(BF16)

    hn = hn_ref[...]
    for c in range(nch):
        u = _dot(hn, win_ref[c])
        xbe_ref[c, halo:halo + rows, :] = u[:, :cw]
        y = jax.nn.gelu(u[:, cw:])
        xc = cb_ref[c]
        for j in range(CONV_WIDTH):
            xc = xc + cw_ref[c, j:j + 1, :] * xbe_ref[c, j * nb:j * nb + rows, :]
        xbe_ref[c, 0:halo, :] = xbe_ref[c, rows:rows + halo, :]
        gt = _dot(xc.astype(BF16), wg_ref[c])
        r = jax.nn.sigmoid(gt[:, :cw] + bra_ref[c])
        ig = jax.nn.sigmoid(gt[:, cw:] + bix_ref[c])
        log_a = (-RGLRU_C) * r * jax.nn.softplus(-lam_ref[c])
        a = jnp.exp(log_a)
        bx = jnp.sqrt(jnp.tanh(-log_a) * (1.0 + a * a)) * (ig * xc)
        h = h_ref[c]
        for t in range(ts):
            sl = slice(t * nb, (t + 1) * nb)
            h = a[sl] * h + bx[sl]
            ry_ref[sl, c * cw:(c + 1) * cw] = (h * y[sl]).astype(BF16)
        h_ref[c] = h
    o_ref[...] = _dot(ry_ref[...], wout_ref[...]) + xin_ref[...]


def _rglru_mixer(x, g, w_in, conv_w, conv_b, w_ra, b_ra, w_ix, b_ix, lam, w_out, *, nb, s, in_bm):
    d = w_in.shape[0]
    r = w_in.shape[1] // 2
    nblk, rb = w_ra.shape[0], w_ra.shape[1]
    per = LANES // math.gcd(rb, LANES)
    cw = per * rb
    nch = nblk // per
    assert nch * per == nblk and nch * cw == r
    ts = 16
    assert s % ts == 0 and nb % 16 == 0
    rows = ts * nb

    def chunks(v):
        return v.reshape(v.shape[0], nch, cw).transpose(1, 0, 2)

    def blockdiag(w):
        w = w.reshape(nch, per, rb, rb)
        return jnp.einsum('cpij,pq->cpiqj', w, jnp.eye(per, dtype=w.dtype)).reshape(nch, cw, cw)

    win = jnp.concatenate([chunks(w_in[:, :r]), chunks(w_in[:, r:])], axis=-1).astype(BF16)
    wg = jnp.concatenate([blockdiag(w_ra), blockdiag(w_ix)], axis=-1).astype(BF16)
    args = (x, g.reshape(1, d), win, chunks(conv_w), chunks(conv_b.reshape(1, r)), wg,
            chunks(b_ra.reshape(1, r)), chunks(b_ix.reshape(1, r)), chunks(lam.reshape(1, r)),
            w_out.astype(BF16))
    x_spec = (pl.BlockSpec((nb, ts, d), lambda i: (0, i, 0)) if in_bm
              else pl.BlockSpec((rows, d), lambda i: (i, 0)))
    in_specs = [x_spec] + [_const_spec(a.shape) for a in args[1:]]
    kern = functools.partial(_rglru_kernel, ts=ts, nb=nb, cw=cw, nch=nch, in_bm=in_bm)
    return pl.pallas_call(
        kern,
        grid=(s // ts,),
        in_specs=in_specs,
        out_specs=pl.BlockSpec((rows, d), lambda i: (i, 0)),
        out_shape=jax.ShapeDtypeStruct((s * nb, d), F32),
        scratch_shapes=[
            pltpu.VMEM((nch, rows + (CONV_WIDTH - 1) * nb, cw), F32),
            pltpu.VMEM((nch, nb, cw), F32),
            pltpu.VMEM((rows, d), BF16),
            pltpu.VMEM((rows, d), F32),
            pltpu.VMEM((rows, r), BF16),
        ],
        compiler_params=_cparams(("arbitrary",)),
        name="rglru_mixer",
    )(*args)


def _ffn_kernel(*refs, f, fc, pre_attn, final_norm, out_bm, nb):
    it = iter(refs)
    h_ref = next(it)
    if pre_attn:
        attn_ref, wo_ref = next(it), next(it)
    g_ref, wgu_ref, wdn_ref = next(it), next(it), next(it)
    if final_norm:
        gf_ref = next(it)
    o_ref = next(it)

    h = h_ref[...]
    if pre_attn:
        h = h + _dot(attn_ref[...], wo_ref[...])
    hn = _rms(h, g_ref[...]).astype(BF16)
    acc = h
    for c in range(f // fc):
        gate = _dot(hn, wgu_ref[:, c * fc:(c + 1) * fc])
        val = _dot(hn, wgu_ref[:, f + c * fc:f + (c + 1) * fc])
        act = (jax.nn.silu(gate) * val).astype(BF16)
        acc = acc + _dot(act, wdn_ref[c * fc:(c + 1) * fc, :])
    if final_norm:
        acc = _rms(acc, gf_ref[...])
    if out_bm:
        for t in range(acc.shape[0] // nb):
            o_ref[:, t, :] = acc[t * nb:(t + 1) * nb, :]
    else:
        o_ref[...] = acc


def _ffn(h, g, w_gu, w_down, *, attn=None, w_o=None, final_g=None, out_bm_nb=None):
    n, d = h.shape
    f = w_down.shape[0]
    tr = 512
    fc = f // 2 if (f // 2) % LANES == 0 else f
    assert n % tr == 0 and f % fc == 0
    pre_attn = attn is not None
    final_norm = final_g is not None
    out_bm = out_bm_nb is not None
    nb = out_bm_nb if out_bm else 0
    row_spec = lambda w: pl.BlockSpec((tr, w), lambda i: (i, 0))
    args, specs = [h], [row_spec(d)]
    if pre_attn:
        args += [attn, w_o.astype(BF16)]
        specs += [row_spec(attn.shape[1]), _const_spec(w_o.shape)]
    args += [g.reshape(1, d), w_gu.astype(BF16), w_down.astype(BF16)]
    specs += [_const_spec((1, d)), _const_spec(w_gu.shape), _const_spec(w_down.shape)]
    if final_norm:
        args.append(final_g.reshape(1, d))
        specs.append(_const_spec((1, d)))
    if out_bm:
        assert tr % nb == 0
        ts = tr // nb
        out_shape = jax.ShapeDtypeStruct((nb, n // nb, d), F32)
        out_spec = pl.BlockSpec((nb, ts, d), lambda i: (0, i, 0))
    else:
        out_shape = jax.ShapeDtypeStruct((n, d), F32)
        out_spec = row_spec(d)
    kern = functools.partial(_ffn_kernel, f=f, fc=fc, pre_attn=pre_attn, final_norm=final_norm,
                             out_bm=out_bm, nb=nb)
    return pl.pallas_call(
        kern, grid=(n // tr,), in_specs=specs, out_specs=out_spec, out_shape=out_shape,
        compiler_params=_cparams(("parallel",)), name="ffn",
    )(*args)


def _rope_slot(x, cos, sin_lo, sin_hi):
    return (x * cos + pltpu.roll(x, SLOT - ROPE_HALF, 1) * sin_lo
            + pltpu.roll(x, ROPE_HALF, 1) * sin_hi)


def _kv_proj_kernel(h_ref, g_ref, w_ref, cos_ref, slo_ref, shi_ref,
                    xk_ref, xv_ref, ks_ref, vs_ref, kw_ref, vw_ref, sc_ref, *, tr, s):
    g_ = N_KV_GROUPS
    hn = _rms(h_ref[...], g_ref[...]).astype(BF16)
    u = _dot(hn, w_ref[...])
    slot = lambda i: u[:, i * SLOT:(i + 1) * SLOT]
    cos, slo, shi = cos_ref[...], slo_ref[...], shi_ref[...]
    pos0 = (pl.program_id(0) % (s // tr)) * tr
    pos = pos0 + lax.broadcasted_iota(jnp.int32, (tr, SLOT), 0)
    lane = lax.broadcasted_iota(jnp.int32, (tr, SLOT), 1)
    onehot = (lane == pos // SEL_BLOCK).astype(F32)

    nj = tr // CMP_STRIDE
    for i in range(2 * g_):
        sc_ref[i] = slot(i)
    for x_ref, base in ((xk_ref, 0), (xv_ref, g_)):
        for gi in range(g_):
            for l in range(CMP_STRIDE):
                x_ref[gi, :, l * SLOT:(l + 1) * SLOT] = sc_ref[base + gi, pl.ds(l, nj, stride=CMP_STRIDE), :]

    for k_ref, v_ref, base, aug in ((ks_ref, vs_ref, 2 * g_, True), (kw_ref, vw_ref, 4 * g_, False)):
        for gi in range(g_):
            k = _rope_slot(slot(base + gi), cos, slo, shi)
            if aug:
                k = k + onehot
            k_ref[:, gi * SLOT:(gi + 1) * SLOT] = k.astype(BF16)
        v_ref[...] = u[:, (base + g_) * SLOT:(base + 2 * g_) * SLOT].astype(BF16)


def _q_proj_kernel(h_ref, g_ref, w_ref, gb_ref, cos_ref, slo_ref, shi_ref, qn_ref, qr_ref, gt_ref):
    hn = _rms(h_ref[...], g_ref[...]).astype(BF16)
    u = _dot(hn, w_ref[...])
    cos, slo, shi = cos_ref[...], slo_ref[...], shi_ref[...]
    scale = D_QK ** -0.5
    for hd in range(N_HEADS):
        q = u[:, hd * SLOT:(hd + 1) * SLOT] * scale
        qn_ref[:, hd * SLOT:(hd + 1) * SLOT] = q.astype(BF16)
        qr_ref[:, hd * SLOT:(hd + 1) * SLOT] = _rope_slot(q, cos, slo, shi).astype(BF16)
    gt_ref[...] = jax.nn.sigmoid(u[:, N_HEADS * SLOT:] + gb_ref[...])


def _pad_slots(w, n, width, left):
    d = w.shape[0]
    w = w.reshape(d, n, width)
    return jnp.pad(w, ((0, 0), (0, 0), (left, SLOT - width - left))).reshape(d, n * SLOT)


def _rope_tables(s):
    inv = ROPE_THETA ** (-jnp.arange(ROPE_HALF, dtype=F32) * 2.0 / ROPE_DIMS)
    ang = jnp.arange(s, dtype=F32)[:, None] * inv[None, :]
    cos, sin = jnp.cos(ang), jnp.sin(ang)
    z = lambda n: jnp.zeros((s, n), F32)
    o = lambda n: jnp.ones((s, n), F32)
    rest = SLOT - AUG - ROPE_DIMS
    cos_t = jnp.concatenate([o(AUG), cos, cos, o(rest)], axis=1)
    sin_lo = jnp.concatenate([z(AUG), -sin, z(ROPE_HALF), z(rest)], axis=1)
    sin_hi = jnp.concatenate([z(AUG), z(ROPE_HALF), sin, z(rest)], axis=1)
    return cos_t, sin_lo, sin_hi


def _kv_proj(h, g, kv_w, tables, *, nb, s):
    n, d = h.shape
    g_, dk, dv = N_KV_GROUPS, D_QK, D_V
    tr = 512
    assert n % tr == 0 and s % tr == 0 and tr % CMP_STRIDE == 0
    sizes = [g_ * dk, g_ * dv] * 3
    offs = np.concatenate([[0], np.cumsum(sizes)])
    kc, vc, ksl, vsl, kwn, vwn = [kv_w[:, offs[j]:offs[j + 1]] for j in range(6)]
    dup = lambda v: jnp.tile(v.reshape(d, g_, 1, dv), (1, 1, SLOT // dv, 1)).reshape(d, g_ * SLOT)
    w = jnp.concatenate([_pad_slots(kc, g_, dk, 0), _pad_slots(vc, g_, dv, 0),
                         _pad_slots(ksl, g_, dk, AUG), dup(vsl),
                         _pad_slots(kwn, g_, dk, AUG), dup(vwn)], axis=1).astype(BF16)
    row = lambda wd, dt: (pl.BlockSpec((tr, wd), lambda i: (i, 0)), jax.ShapeDtypeStruct((n, wd), dt))
    nj = tr // CMP_STRIDE
    spt = s // tr
    xcmp = (pl.BlockSpec((None, g_, nj, CMP_STRIDE * SLOT), lambda i: (i // spt, 0, i % spt, 0)),
            jax.ShapeDtypeStruct((nb, g_, s // CMP_STRIDE, CMP_STRIDE * SLOT), F32))
    outs = [xcmp, xcmp, row(g_ * SLOT, BF16), row(g_ * SLOT, BF16), row(g_ * SLOT, BF16), row(g_ * SLOT, BF16)]
    tab_spec = pl.BlockSpec((tr, SLOT), lambda i: (i % spt, 0))
    return pl.pallas_call(
        functools.partial(_kv_proj_kernel, tr=tr, s=s),
        grid=(n // tr,),
        in_specs=[pl.BlockSpec((tr, d), lambda i: (i, 0)), _const_spec((1, d)), _const_spec(w.shape),
                  tab_spec, tab_spec, tab_spec],
        out_specs=[o[0] for o in outs], out_shape=[o[1] for o in outs],
        scratch_shapes=[pltpu.VMEM((2 * g_, tr, SLOT), F32)],
        compiler_params=_cparams(("parallel",)), name="kv_proj",
    )(h, g.reshape(1, d), w, *tables)


def _q_proj(h, g, w_q, gate_bias, tables, *, s):
    n, d = h.shape
    tr = 512
    nq = N_HEADS * D_QK
    ngate = N_HEADS * N_BRANCH
    assert ngate <= LANES
    w = jnp.concatenate([_pad_slots(w_q[:, :nq], N_HEADS, D_QK, AUG),
                         jnp.pad(w_q[:, nq:], ((0, 0), (0, LANES - ngate)))], axis=1).astype(BF16)
    gb = jnp.pad(gate_bias.reshape(1, ngate), ((0, 0), (0, LANES - ngate)))
    tab_spec = pl.BlockSpec((tr, SLOT), lambda i: (i % (s // tr), 0))
    row = lambda wd, dt: (pl.BlockSpec((tr, wd), lambda i: (i, 0)), jax.ShapeDtypeStruct((n, wd), dt))
    outs = [row(N_HEADS * SLOT, BF16), row(N_HEADS * SLOT, BF16), row(LANES, F32)]
    return pl.pallas_call(
        _q_proj_kernel,
        grid=(n // tr,),
        in_specs=[pl.BlockSpec((tr, d), lambda i: (i, 0)), _const_spec((1, d)), _const_spec(w.shape),
                  _const_spec((1, LANES)), tab_spec, tab_spec, tab_spec],
        out_specs=[o[0] for o in outs], out_shape=[o[1] for o in outs],
        compiler_params=_cparams(("parallel",)), name="q_proj",
    )(h, g.reshape(1, d), w, gb, *tables)


def _compress_kernel(x_ref, plo_ref, phi_ref, w1lo_ref, w1hi_ref, w2_ref, o_ref):
    x = x_ref[...]
    lo = _dot((x + plo_ref[...]).astype(BF16), w1lo_ref[...])
    hi = _dot((x + phi_ref[...]).astype(BF16), w1hi_ref[...])
    pre = lo + pltpu.roll(hi, hi.shape[0] - 1, 0)
    o_ref[...] = _dot(jax.nn.gelu(pre).astype(BF16), w2_ref[...]).astype(BF16)


def _compress(x, pos, w1, w2_slot, *, nchunk):
    n, width = x.shape
    dh = pos.shape[1]
    hid = w1.shape[1]
    assert CMP_BLOCK == 2 * CMP_STRIDE and width == CMP_STRIDE * SLOT
    tr = 4 * nchunk
    assert n % tr == 0

    def slots(v):
        k = v.shape[1]
        v = jnp.pad(v.reshape(2, CMP_STRIDE, dh, k), ((0, 0), (0, 0), (0, SLOT - dh), (0, 0)))
        return v.reshape(2, width, k)

    p = slots(pos.reshape(CMP_BLOCK * dh, 1))[:, :, 0].reshape(2, 1, width)
    w1s = slots(w1).astype(BF16)
    return pl.pallas_call(
        _compress_kernel,
        grid=(n // tr,),
        in_specs=[pl.BlockSpec((tr, width), lambda i: (i, 0)), _const_spec((1, width)), _const_spec((1, width)),
                  _const_spec((width, hid)), _const_spec((width, hid)), _const_spec((hid, SLOT))],
        out_specs=pl.BlockSpec((tr, SLOT), lambda i: (i, 0)),
        out_shape=jax.ShapeDtypeStruct((n, SLOT), BF16),
        compiler_params=_cparams(("parallel",)), name="compress",
    )(x, p[0], p[1], w1s[0], w1s[1], w2_slot.astype(BF16))


def _split3(x):
    hi = x.astype(BF16)
    r1 = x - hi.astype(F32)
    mid = r1.astype(BF16)
    lo = (r1 - mid.astype(F32)).astype(BF16)
    return hi, mid, lo


def _nsa_kernel(qn_ref, qr_ref, gt_ref, kc_ref, vc_ref, ks_ref, vs_ref, kw_ref, vw_ref, ov_ref, ex_ref,
                o_ref, s_ref, mx_ref, acc_ref, *, nc, ns, ktop):
    g_ = N_KV_GROUPS
    tq = Q_BLOCK
    rows = HPG * tq
    kch = 2 * tq
    wlen = WINDOW + tq
    qi = pl.program_id(1)
    s0 = qi * tq
    wstart = pl.multiple_of(jnp.maximum(s0 - WINDOW, 0), tq)
    hw = N_HEADS * D_V

    ghi, gmid, _ = _split3(gt_ref[...])
    gx = _dot(ghi, ex_ref[...]) + _dot(gmid, ex_ref[...])

    r_row = lax.broadcasted_iota(jnp.int32, (rows, 1), 0) & (tq - 1)
    lane = lax.broadcasted_iota(jnp.int32, (tq, SLOT), 1)
    stack = lambda ref, g: [ref[:, hd * SLOT:(hd + 1) * SLOT] for hd in range(g * HPG, (g + 1) * HPG)]

    o_c, psums = [], []
    for g in range(g_):
        st = _dot_nt(kc_ref[g], jnp.concatenate(stack(qn_ref, g), axis=0))
        c_idx = lax.broadcasted_iota(jnp.int32, st.shape, 0)
        t_col = s0 + (lax.broadcasted_iota(jnp.int32, st.shape, 1) & (tq - 1))
        valid = (c_idx * CMP_STRIDE + (CMP_BLOCK - 1) <= t_col) & (c_idx < nc)
        st = jnp.where(valid, st, NEG)
        e = jnp.where(valid, jnp.exp(st - jnp.max(st, axis=0, keepdims=True)), 0.0)
        den = jnp.sum(e, axis=0, keepdims=True)
        pt = e / jnp.where(den > 0, den, 1.0)
        o_c.append(_dot(pt.T.astype(BF16), vc_ref[g]))
        psum = pt[:, 0:tq]
        for i in range(1, HPG):
            psum = psum + pt[:, i * tq:(i + 1) * tq]
        psums.append(psum)

    psum = jnp.concatenate(psums, axis=1)
    imp = sum(_dot(ov_ref[...], part) for part in _split3(psum))[0:ns, :]
    n_i = lax.broadcasted_iota(jnp.int32, imp.shape, 0)
    tt = s0 + (lax.broadcasted_iota(jnp.int32, imp.shape, 1) & (tq - 1))
    cur = tt // SEL_BLOCK
    forced = (n_i == 0) | (n_i == cur) | (n_i == cur - 1)
    score = jnp.where(forced, jnp.inf, jnp.where(n_i * SEL_BLOCK <= tt, imp, -jnp.inf))
    sel = jnp.zeros(imp.shape, jnp.bool_)
    for _ in range(ktop):
        mx = jnp.max(score, axis=0, keepdims=True)
        idx = jnp.min(jnp.where(score == mx, n_i, ns), axis=0, keepdims=True)
        hit = n_i == idx
        sel = sel | hit
        score = jnp.where(hit, -jnp.inf, score)
    bias_t = jnp.where(sel, 0.0, SEL_BIAS)
    bias_t = jnp.concatenate([bias_t, jnp.zeros((SLOT - ns, g_ * tq), F32)], axis=0)

    qr, qa = [], []
    for g in range(g_):
        bias = bias_t[:, g * tq:(g + 1) * tq].T.astype(BF16)
        heads = stack(qr_ref, g)
        qr.append(jnp.concatenate(heads, axis=0))
        qa.append(jnp.concatenate([q + bias for q in heads], axis=0))

    nchunk = (qi + 2) // 2
    dif = lax.broadcasted_iota(jnp.int32, (rows, kch), 1) - r_row

    def score_chunk(j, causal):
        k0 = pl.multiple_of(j * kch, kch)
        for g in range(g_):
            sc = _dot_nt(qa[g], ks_ref[pl.ds(k0, kch), g * SLOT:(g + 1) * SLOT])
            if causal:
                sc = jnp.where(dif <= s0 - k0, sc, NEG)
            s_ref[g, j] = sc
            mx_ref[g] = jnp.maximum(mx_ref[g], jnp.maximum(sc[:, :SLOT], sc[:, SLOT:]))

    mx_ref[...] = jnp.full(mx_ref.shape, NEG, F32)

    def body1(j, carry):
        score_chunk(j, False)
        return carry

    lax.fori_loop(0, nchunk - 1, body1, 0)
    score_chunk(nchunk - 1, True)
    for g in range(g_):
        mx_ref[g] = jnp.broadcast_to(jnp.max(mx_ref[g], axis=1, keepdims=True), (rows, SLOT))
    acc_ref[...] = jnp.zeros(acc_ref.shape, F32)
    ones_k = jnp.ones((kch, SLOT), BF16)

    def body2(j, carry):
        k0 = pl.multiple_of(j * kch, kch)
        for g in range(g_):
            sc = s_ref[g, j]
            m = mx_ref[g]
            p = jnp.concatenate([jnp.exp(sc[:, :SLOT] - m), jnp.exp(sc[:, SLOT:] - m)], axis=1).astype(BF16)
            v = jnp.concatenate([vs_ref[pl.ds(k0, kch), g * SLOT:(g + 1) * SLOT], ones_k], axis=1)
            acc_ref[g] += _dot(p, v)
        return carry

    lax.fori_loop(0, nchunk, body2, 0)

    ones_w = jnp.ones((wlen, SLOT), BF16)
    wcol = lax.broadcasted_iota(jnp.int32, (rows, wlen), 1) - r_row
    wmask = (wcol <= s0 - wstart) & (wcol > s0 - wstart - WINDOW)
    for g in range(g_):
        acc = acc_ref[g]
        o_s = acc[:, :SLOT] / acc[:, SLOT:]
        sw = _dot_nt(qr[g], kw_ref[pl.ds(wstart, wlen), g * SLOT:(g + 1) * SLOT])
        sw = jnp.where(wmask, sw, NEG)
        pw = jnp.exp(sw - jnp.max(sw, axis=1, keepdims=True)).astype(BF16)
        vwin = jnp.concatenate([vw_ref[pl.ds(wstart, wlen), g * SLOT:(g + 1) * SLOT], ones_w], axis=1)
        accw = _dot(pw, vwin)
        o_w = accw[:, :SLOT] / accw[:, SLOT:]
        for hp in range(HPG // 2):
            col = (g * HPG + 2 * hp) * D_V
            ev, od = slice(2 * hp * tq, (2 * hp + 1) * tq), slice((2 * hp + 1) * tq, (2 * hp + 2) * tq)
            tile = jnp.zeros((tq, SLOT), F32)
            for br, o_b in enumerate((o_c[g], o_s, o_w)):
                pair = jnp.where(lane < D_V, o_b[ev], o_b[od])
                tile = tile + gx[:, br * hw + col:br * hw + col + SLOT] * pair
            o_ref[:, col:col + SLOT] = tile.astype(BF16)


def _nsa_attn(qn, qr, gates, kc, vc, ks, vs, kw, vw, *, nb, s):
    tq = Q_BLOCK
    nq = s // tq
    nc = (s - CMP_BLOCK) // CMP_STRIDE + 1
    ncp = s // CMP_STRIDE
    ns = s // SEL_BLOCK
    ktop = min(SEL_TOPK, ns)
    g_ = N_KV_GROUPS
    assert ncp % 8 == 0 and ns <= AUG and s >= WINDOW + tq and s % (2 * tq) == 0 and 2 * D_V == SLOT
    hw = N_HEADS * D_V
    cs = np.arange(ncp) * CMP_STRIDE
    ss = np.arange(SLOT) * SEL_BLOCK
    ov = ((cs[None, :] < ss[:, None] + SEL_BLOCK) & (cs[None, :] + CMP_BLOCK > ss[:, None])
          & (np.arange(ncp)[None, :] < nc) & (np.arange(SLOT)[:, None] < ns))
    ex = np.zeros((LANES, N_BRANCH * hw), np.float32)
    for hd in range(N_HEADS):
        for br in range(N_BRANCH):
            ex[hd * N_BRANCH + br, br * hw + hd * D_V:br * hw + (hd + 1) * D_V] = 1.0
    qspec = lambda wd: pl.BlockSpec((tq, wd), lambda b, q: (b * nq + q, 0))
    seq = lambda wd: pl.BlockSpec((s, wd), lambda b, q: (b, 0))
    cmp_spec = pl.BlockSpec((None, g_, ncp, SLOT), lambda b, q: (b, 0, 0, 0))
    nkc = s // (2 * tq)
    return pl.pallas_call(
        functools.partial(_nsa_kernel, nc=nc, ns=ns, ktop=ktop),
        grid=(nb, nq),
        in_specs=[qspec(N_HEADS * SLOT), qspec(N_HEADS * SLOT), qspec(LANES), cmp_spec, cmp_spec,
                  seq(g_ * SLOT), seq(g_ * SLOT), seq(g_ * SLOT), seq(g_ * SLOT),
                  _const_spec(ov.shape), _const_spec(ex.shape)],
        out_specs=qspec(hw),
        out_shape=jax.ShapeDtypeStruct((nb * s, hw), BF16),
        scratch_shapes=[pltpu.VMEM((g_, nkc, HPG * tq, 2 * tq), F32),
                        pltpu.VMEM((g_, HPG * tq, SLOT), F32),
                        pltpu.VMEM((g_, HPG * tq, 2 * SLOT), F32)],
        compiler_params=_cparams(("parallel", "arbitrary")), name="nsa_attn",
    )(qn, qr, gates, kc.reshape(nb, g_, ncp, SLOT), vc.reshape(nb, g_, ncp, SLOT), ks, vs, kw, vw,
      jnp.asarray(ov, BF16), jnp.asarray(ex, BF16))


def kernel(x, norm_g, ffn_w_gu, ffn_w_down, a_w_in, a_conv_w, a_conv_b, a_w_ra, a_b_ra, a_w_ix, a_b_ix,
           a_lambda, a_w_out, kv_norm_g, kv_w, cmp_pos_k, cmp_w1_k, cmp_w2_k, cmp_pos_v, cmp_w1_v,
           cmp_w2_v, b_w_q, b_gate_bias, b_w_o, final_g):
    nb, s, d = x.shape
    n_a, n_b = a_w_in.shape[0], b_w_q.shape[0]
    depth = n_a + n_b
    assert n_a >= 1 and norm_g.shape[0] == depth

    h = x
    for layer in range(n_a):
        hm = _rglru_mixer(h, norm_g[layer, 0], a_w_in[layer], a_conv_w[layer], a_conv_b[layer],
                          a_w_ra[layer], a_b_ra[layer], a_w_ix[layer], a_b_ix[layer], a_lambda[layer],
                          a_w_out[layer], nb=nb, s=s, in_bm=(layer == 0))
        last = layer == n_a - 1
        h = _ffn(hm, norm_g[layer, 1], ffn_w_gu[layer], ffn_w_down[layer],
                 final_g=final_g if layer == depth - 1 else None,
                 out_bm_nb=nb if last else None)
    if n_b == 0:
        return h
    h = h.reshape(nb * s, d)

    tables = _rope_tables(s)
    xk, xv, ks, vs, kw, vw = _kv_proj(h, kv_norm_g, kv_w, tables, nb=nb, s=s)
    nchunk = s // CMP_STRIDE
    w2k = jnp.pad(cmp_w2_k, ((0, 0), (AUG, 0)))
    w2v = jnp.tile(cmp_w2_v, (1, SLOT // D_V))
    kc = _compress(xk.reshape(-1, xk.shape[-1]), cmp_pos_k, cmp_w1_k, w2k, nchunk=nchunk)
    vc = _compress(xv.reshape(-1, xv.shape[-1]), cmp_pos_v, cmp_w1_v, w2v, nchunk=nchunk)

    for j in range(n_b):
        layer = n_a + j
        qn, qr, gates = _q_proj(h, norm_g[layer, 0], b_w_q[j], b_gate_bias[j], tables, s=s)
        attn = _nsa_attn(qn, qr, gates, kc, vc, ks, vs, kw, vw, nb=nb, s=s)
        h = _ffn(h, norm_g[layer, 1], ffn_w_gu[layer], ffn_w_down[layer], attn=attn, w_o=b_w_o[j],
                 final_g=final_g if layer == depth - 1 else None)
    return h.reshape(nb, s, d)
```

```python
import functools
import math

import numpy as np
import jax
import jax.numpy as jnp
from jax import lax
from jax.experimental import pallas as pl
from jax.experimental.pallas import tpu as pltpu

CONV_WIDTH = 4
RGLRU_C = 8.0
N_HEADS = 16
N_KV_GROUPS = 4
HPG = N_HEADS // N_KV_GROUPS
D_QK = 96
D_V = 64
ROPE_DIMS = D_QK // 4
ROPE_HALF = ROPE_DIMS // 2
ROPE_THETA = 500000.0
CMP_BLOCK = 32
CMP_STRIDE = 16
SEL_BLOCK = 64
SEL_TOPK = 8
WINDOW = 512
Q_BLOCK = 128
N_BRANCH = 3
EPS = 1e-6

LANES = 128
SLOT = 128
AUG = SLOT - D_QK
VMEM_LIMIT = 56 * 1024 * 1024
NEG = -1e30
SEL_BIAS = -1e9
LOG2E = math.log2(math.e)

F32 = jnp.float32
BF16 = jnp.bfloat16


def _cparams(sem):
    return pltpu.CompilerParams(dimension_semantics=sem, vmem_limit_bytes=VMEM_LIMIT)


def _const_spec(shape):
    nd = len(shape)
    return pl.BlockSpec(shape, lambda *_: (0,) * nd, pipeline_mode=pl.Buffered(1))


def _rms(x, g):
    return x * lax.rsqrt(jnp.mean(x * x, axis=-1, keepdims=True) + EPS) * g


def _gelu(x):
    k1 = -2.0 * math.sqrt(2.0 / math.pi) * LOG2E
    return x / (1.0 + jnp.exp2(x * (k1 + (k1 * 0.044715) * (x * x))))


def _dot(a, b):
    return jnp.dot(a, b, preferred_element_type=F32)


def _dot_nt(a, b):
    return lax.dot_general(a, b, (((1,), (1,)), ((), ())), preferred_element_type=F32)


def _rglru_kernel(x_ref, g_ref, perm_ref, permt_ref, win_ref, cw_ref, cb_ref, wg_ref, bra_ref, bix_ref,
                  lam_ref, wout_ref, o_ref, xbe_ref, h_ref, ry_ref, *, ts, nb, cw, nch):
    rows = ts * nb
    halo = (CONV_WIDTH - 1) * nb
    d = x_ref.shape[-1]

    @pl.when(pl.program_id(0) == 0)
    def _():
        xbe_ref[:, 0:halo, :] = jnp.zeros((nch, halo, cw), F32)
        h_ref[...] = jnp.zeros_like(h_ref)

    hn_bt = _rms(x_ref[...].reshape(rows, d), g_ref[...]).astype(BF16)
    hn = _dot(perm_ref[...], hn_bt).astype(BF16)
    for c in range(nch):
        u = _dot(hn, win_ref[c])
        xbe_ref[c, halo:halo + rows, :] = u[:, :cw]
        y = _gelu(u[:, cw:])
        xc = cb_ref[c]
        for j in range(CONV_WIDTH):
            xc = xc + cw_ref[c, j:j + 1, :] * xbe_ref[c, j * nb:j * nb + rows, :]
        xbe_ref[c, 0:halo, :] = xbe_ref[c, rows:rows + halo, :]
        gt = _dot(xc.astype(BF16), wg_ref[c])
        r = jax.nn.sigmoid(gt[:, :cw] + bra_ref[c])
        ig = jax.nn.sigmoid(gt[:, cw:] + bix_ref[c])
        nla = r * (RGLRU_C * jax.nn.softplus(-lam_ref[c]))
        a = jnp.exp2(nla * (-LOG2E))
        om = jnp.tanh(nla) * (1.0 + a * a)
        bx = jnp.where(om > 0, om * lax.rsqrt(om), 0.0) * (ig * xc)
        h = h_ref[c]
        for t in range(ts):
            sl = slice(t * nb, (t + 1) * nb)
            h = a[sl] * h + bx[sl]
            ry_ref[sl, c * cw:(c + 1) * cw] = (h * y[sl]).astype(BF16)
        h_ref[c] = h
    ry_bt = _dot(permt_ref[...], ry_ref[...]).astype(BF16)
    out = _dot(ry_bt, wout_ref[...]) + x_ref[...].reshape(rows, d)
    o_ref[...] = out.reshape(nb, ts, d)


def _rglru_mixer(x, g, w_in, conv_w, conv_b, w_ra, b_ra, w_ix, b_ix, lam, w_out):
    nb, s, _ = x.shape
    d = w_in.shape[0]
    r = w_in.shape[1] // 2
    nblk, rb = w_ra.shape[0], w_ra.shape[1]
    per = LANES // math.gcd(rb, LANES)
    cw = per * rb
    nch = nblk // per
    assert nch * per == nblk and nch * cw == r
    ts = 16
    assert s % ts == 0 and nb % 16 == 0
    rows = ts * nb

    def chunks(v):
        return v.reshape(v.shape[0], nch, cw).transpose(1, 0, 2)

    def blockdiag(w):
        w = w.reshape(nch, per, rb, rb)
        return jnp.einsum('cpij,pq->cpiqj', w, jnp.eye(per, dtype=w.dtype)).reshape(nch, cw, cw)

    win = jnp.concatenate([chunks(w_in[:, :r]), chunks(w_in[:, r:])], axis=-1).astype(BF16)
    wg = jnp.concatenate([blockdiag(w_ra), blockdiag(w_ix)], axis=-1).astype(BF16)
    src = (np.arange(rows) % nb) * ts + np.arange(rows) // nb
    perm = np.zeros((rows, rows), np.float32)
    perm[np.arange(rows), src] = 1.0
    args = (x, g.reshape(1, d), jnp.asarray(perm, BF16), jnp.asarray(perm.T, BF16), win,
            chunks(conv_w), chunks(conv_b.reshape(1, r)), wg,
            chunks(b_ra.reshape(1, r)), chunks(b_ix.reshape(1, r)), chunks(lam.reshape(1, r)),
            w_out.astype(BF16))
    blk = pl.BlockSpec((nb, ts, d), lambda i: (0, i, 0))
    in_specs = [blk] + [_const_spec(a.shape) for a in args[1:]]
    kern = functools.partial(_rglru_kernel, ts=ts, nb=nb, cw=cw, nch=nch)
    return pl.pallas_call(
        kern,
        grid=(s // ts,),
        in_specs=in_specs,
        out_specs=blk,
        out_shape=jax.ShapeDtypeStruct((nb, s, d), F32),
        scratch_shapes=[
            pltpu.VMEM((nch, rows + (CONV_WIDTH - 1) * nb, cw), F32),
            pltpu.VMEM((nch, nb, cw), F32),
            pltpu.VMEM((rows, r), BF16),
        ],
        compiler_params=_cparams(("arbitrary",)),
        name="rglru_mixer",
    )(*args)


def _ffn_kernel(*refs, f, fc, pre_attn, final_norm):
    it = iter(refs)
    h_ref = next(it)
    if pre_attn:
        attn_ref, wo_ref = next(it), next(it)
    g_ref, wgu_ref, wdn_ref = next(it), next(it), next(it)
    if final_norm:
        gf_ref = next(it)
    o_ref = next(it)

    h = h_ref[...]
    if pre_attn:
        h = h + _dot(attn_ref[...], wo_ref[...])
    hn = _rms(h, g_ref[...]).astype(BF16)
    acc = h
    for c in range(f // fc):
        gate = _dot(hn, wgu_ref[:, c * fc:(c + 1) * fc])
        val = _dot(hn, wgu_ref[:, f + c * fc:f + (c + 1) * fc])
        act = (jax.nn.silu(gate) * val).astype(BF16)
        acc = acc + _dot(act, wdn_ref[c * fc:(c + 1) * fc, :])
    if final_norm:
        acc = _rms(acc, gf_ref[...])
    o_ref[...] = acc


def _ffn(h, g, w_gu, w_down, *, attn=None, w_o=None, final_g=None):
    n, d = h.shape
    f = w_down.shape[0]
    tr = 512
    fc = f // 2 if (f // 2) % LANES == 0 else f
    assert n % tr == 0 and f % fc == 0
    pre_attn = attn is not None
    final_norm = final_g is not None
    row_spec = lambda w: pl.BlockSpec((tr, w), lambda i: (i, 0))
    args, specs = [h], [row_spec(d)]
    if pre_attn:
        args += [attn, w_o.astype(BF16)]
        specs += [row_spec(attn.shape[1]), _const_spec(w_o.shape)]
    args += [g.reshape(1, d), w_gu.astype(BF16), w_down.astype(BF16)]
    specs += [_const_spec((1, d)), _const_spec(w_gu.shape), _const_spec(w_down.shape)]
    if final_norm:
        args.append(final_g.reshape(1, d))
        specs.append(_const_spec((1, d)))
    kern = functools.partial(_ffn_kernel, f=f, fc=fc, pre_attn=pre_attn, final_norm=final_norm)
    return pl.pallas_call(
        kern, grid=(n // tr,), in_specs=specs, out_specs=row_spec(d),
        out_shape=jax.ShapeDtypeStruct((n, d), F32),
        compiler_params=_cparams(("parallel",)), name="ffn",
    )(*args)


def _rope_slot(x, cos, sin_lo, sin_hi):
    return (x * cos + pltpu.roll(x, SLOT - ROPE_HALF, 1) * sin_lo
            + pltpu.roll(x, ROPE_HALF, 1) * sin_hi)


def _kv_proj_kernel(h_ref, g_ref, w_ref, cos_ref, slo_ref, shi_ref,
                    xk_ref, xv_ref, ks_ref, vs_ref, kw_ref, vw_ref, sc_ref, *, tr, s):
    g_ = N_KV_GROUPS
    hn = _rms(h_ref[...], g_ref[...]).astype(BF16)
    u = _dot(hn, w_ref[...])
    slot = lambda i: u[:, i * SLOT:(i + 1) * SLOT]
    cos, slo, shi = cos_ref[...], slo_ref[...], shi_ref[...]
    pos0 = (pl.program_id(0) % (s // tr)) * tr
    pos = pos0 + lax.broadcasted_iota(jnp.int32, (tr, SLOT), 0)
    lane = lax.broadcasted_iota(jnp.int32, (tr, SLOT), 1)
    onehot = (lane == pos // SEL_BLOCK).astype(F32)

    nj = tr // CMP_STRIDE
    for i in range(2 * g_):
        sc_ref[i] = slot(i)
    for x_ref, base in ((xk_ref, 0), (xv_ref, g_)):
        for gi in range(g_):
            for l in range(CMP_STRIDE):
                x_ref[gi, :, l * SLOT:(l + 1) * SLOT] = sc_ref[base + gi, pl.ds(l, nj, stride=CMP_STRIDE), :]

    for k_ref, v_ref, base, aug in ((ks_ref, vs_ref, 2 * g_, True), (kw_ref, vw_ref, 4 * g_, False)):
        for gi in range(g_):
            k = _rope_slot(slot(base + gi), cos, slo, shi)
            if aug:
                k = k + onehot
            k_ref[:, gi * SLOT:(gi + 1) * SLOT] = k.astype(BF16)
        v_ref[...] = u[:, (base + g_) * SLOT:(base + 2 * g_) * SLOT].astype(BF16)


def _q_proj_kernel(h_ref, g_ref, w_ref, gb_ref, cos_ref, slo_ref, shi_ref, qn_ref, qr_ref, gt_ref):
    hn = _rms(h_ref[...], g_ref[...]).astype(BF16)
    u = _dot(hn, w_ref[...])
    cos, slo, shi = cos_ref[...], slo_ref[...], shi_ref[...]
    scale = D_QK ** -0.5 * LOG2E
    for hd in range(N_HEADS):
        q = u[:, hd * SLOT:(hd + 1) * SLOT] * scale
        qn_ref[:, hd * SLOT:(hd + 1) * SLOT] = q.astype(BF16)
        qr_ref[:, hd * SLOT:(hd + 1) * SLOT] = _rope_slot(q, cos, slo, shi).astype(BF16)
    gt_ref[...] = jax.nn.sigmoid(u[:, N_HEADS * SLOT:] + gb_ref[...])


def _pad_slots(w, n, width, left):
    d = w.shape[0]
    w = w.reshape(d, n, width)
    return jnp.pad(w, ((0, 0), (0, 0), (left, SLOT - width - left))).reshape(d, n * SLOT)


def _rope_tables(s):
    inv = ROPE_THETA ** (-jnp.arange(ROPE_HALF, dtype=F32) * 2.0 / ROPE_DIMS)
    ang = jnp.arange(s, dtype=F32)[:, None] * inv[None, :]
    cos, sin = jnp.cos(ang), jnp.sin(ang)
    z = lambda n: jnp.zeros((s, n), F32)
    o = lambda n: jnp.ones((s, n), F32)
    rest = SLOT - AUG - ROPE_DIMS
    cos_t = jnp.concatenate([o(AUG), cos, cos, o(rest)], axis=1)
    sin_lo = jnp.concatenate([z(AUG), -sin, z(ROPE_HALF), z(rest)], axis=1)
    sin_hi = jnp.concatenate([z(AUG), z(ROPE_HALF), sin, z(rest)], axis=1)
    return cos_t, sin_lo, sin_hi


def _kv_proj(h, g, kv_w, tables, *, nb, s):
    n, d = h.shape
    g_, dk, dv = N_KV_GROUPS, D_QK, D_V
    tr = 512
    assert n % tr == 0 and s % tr == 0 and tr % CMP_STRIDE == 0
    sizes = [g_ * dk, g_ * dv] * 3
    offs = np.concatenate([[0], np.cumsum(sizes)])
    kc, vc, ksl, vsl, kwn, vwn = [kv_w[:, offs[j]:offs[j + 1]] for j in range(6)]
    dup = lambda v: jnp.tile(v.reshape(d, g_, 1, dv), (1, 1, SLOT // dv, 1)).reshape(d, g_ * SLOT)
    w = jnp.concatenate([_pad_slots(kc, g_, dk, 0), _pad_slots(vc, g_, dv, 0),
                         _pad_slots(ksl, g_, dk, AUG), dup(vsl),
                         _pad_slots(kwn, g_, dk, AUG), dup(vwn)], axis=1).astype(BF16)
    row = lambda wd, dt: (pl.BlockSpec((tr, wd), lambda i: (i, 0)), jax.ShapeDtypeStruct((n, wd), dt))
    nj = tr // CMP_STRIDE
    spt = s // tr
    xcmp = (pl.BlockSpec((None, g_, nj, CMP_STRIDE * SLOT), lambda i: (i // spt, 0, i % spt, 0)),
            jax.ShapeDtypeStruct((nb, g_, s // CMP_STRIDE, CMP_STRIDE * SLOT), F32))
    outs = [xcmp, xcmp, row(g_ * SLOT, BF16), row(g_ * SLOT, BF16), row(g_ * SLOT, BF16), row(g_ * SLOT, BF16)]
    tab_spec = pl.BlockSpec((tr, SLOT), lambda i: (i % spt, 0))
    return pl.pallas_call(
        functools.partial(_kv_proj_kernel, tr=tr, s=s),
        grid=(n // tr,),
        in_specs=[pl.BlockSpec((tr, d), lambda i: (i, 0)), _const_spec((1, d)), _const_spec(w.shape),
                  tab_spec, tab_spec, tab_spec],
        out_specs=[o[0] for o in outs], out_shape=[o[1] for o in outs],
        scratch_shapes=[pltpu.VMEM((2 * g_, tr, SLOT), F32)],
        compiler_params=_cparams(("parallel",)), name="kv_proj",
    )(h, g.reshape(1, d), w, *tables)


def _q_proj(h, g, w_q, gate_bias, tables, *, s):
    n, d = h.shape
    tr = 512
    nq = N_HEADS * D_QK
    ngate = N_HEADS * N_BRANCH
    assert ngate <= LANES
    w = jnp.concatenate([_pad_slots(w_q[:, :nq], N_HEADS, D_QK, AUG),
                         jnp.pad(w_q[:, nq:], ((0, 0), (0, LANES - ngate)))], axis=1).astype(BF16)
    gb = jnp.pad(gate_bias.reshape(1, ngate), ((0, 0), (0, LANES - ngate)))
    tab_spec = pl.BlockSpec((tr, SLOT), lambda i: (i % (s // tr), 0))
    row = lambda wd, dt: (pl.BlockSpec((tr, wd), lambda i: (i, 0)), jax.ShapeDtypeStruct((n, wd), dt))
    outs = [row(N_HEADS * SLOT, BF16), row(N_HEADS * SLOT, BF16), row(LANES, F32)]
    return pl.pallas_call(
        _q_proj_kernel,
        grid=(n // tr,),
        in_specs=[pl.BlockSpec((tr, d), lambda i: (i, 0)), _const_spec((1, d)), _const_spec(w.shape),
                  _const_spec((1, LANES)), tab_spec, tab_spec, tab_spec],
        out_specs=[o[0] for o in outs], out_shape=[o[1] for o in outs],
        compiler_params=_cparams(("parallel",)), name="q_proj",
    )(h, g.reshape(1, d), w, gb, *tables)


def _compress_kernel(x_ref, plo_ref, phi_ref, w1lo_ref, w1hi_ref, w2_ref, o_ref):
    x = x_ref[...]
    lo = _dot((x + plo_ref[...]).astype(BF16), w1lo_ref[...])
    hi = _dot((x + phi_ref[...]).astype(BF16), w1hi_ref[...])
    pre = lo + pltpu.roll(hi, hi.shape[0] - 1, 0)
    o_ref[...] = _dot(_gelu(pre).astype(BF16), w2_ref[...]).astype(BF16)


def _compress(x, pos, w1, w2_slot, *, nchunk):
    n, width = x.shape
    dh = pos.shape[1]
    hid = w1.shape[1]
    assert CMP_BLOCK == 2 * CMP_STRIDE and width == CMP_STRIDE * SLOT
    tr = 4 * nchunk
    assert n % tr == 0

    def slots(v):
        k = v.shape[1]
        v = jnp.pad(v.reshape(2, CMP_STRIDE, dh, k), ((0, 0), (0, 0), (0, SLOT - dh), (0, 0)))
        return v.reshape(2, width, k)

    p = slots(pos.reshape(CMP_BLOCK * dh, 1))[:, :, 0].reshape(2, 1, width)
    w1s = slots(w1).astype(BF16)
    return pl.pallas_call(
        _compress_kernel,
        grid=(n // tr,),
        in_specs=[pl.BlockSpec((tr, width), lambda i: (i, 0)), _const_spec((1, width)), _const_spec((1, width)),
                  _const_spec((width, hid)), _const_spec((width, hid)), _const_spec((hid, SLOT))],
        out_specs=pl.BlockSpec((tr, SLOT), lambda i: (i, 0)),
        out_shape=jax.ShapeDtypeStruct((n, SLOT), BF16),
        compiler_params=_cparams(("parallel",)), name="compress",
    )(x, p[0], p[1], w1s[0], w1s[1], w2_slot.astype(BF16))


def _split3(x):
    hi = x.astype(BF16)
    r1 = x - hi.astype(F32)
    mid = r1.astype(BF16)
    lo = (r1 - mid.astype(F32)).astype(BF16)
    return hi, mid, lo


def _nsa_kernel(qn_ref, qr_ref, gt_ref, kc_ref, vc_ref, ks_ref, vs_ref, kw_ref, vw_ref, ov_ref, ex_ref,
                o_ref, s_ref, mx_ref, acc_ref, ow_ref, *, nc, ns, ktop):
    g_ = N_KV_GROUPS
    tq = Q_BLOCK
    rows = HPG * tq
    kch = 2 * tq
    wlen = WINDOW + tq
    qi = pl.program_id(1)
    s0 = qi * tq
    wstart = pl.multiple_of(jnp.maximum(s0 - WINDOW, 0), tq)
    hw = N_HEADS * D_V

    ghi, gmid, _ = _split3(gt_ref[...])
    gx = _dot(ghi, ex_ref[...]) + _dot(gmid, ex_ref[...])

    r_row = lax.broadcasted_iota(jnp.int32, (rows, 1), 0) & (tq - 1)
    lane = lax.broadcasted_iota(jnp.int32, (tq, SLOT), 1)
    stack = lambda ref, g: [ref[:, hd * SLOT:(hd + 1) * SLOT] for hd in range(g * HPG, (g + 1) * HPG)]

    o_c, psums = [], []
    for g in range(g_):
        st = _dot_nt(kc_ref[g], jnp.concatenate(stack(qn_ref, g), axis=0))
        c_idx = lax.broadcasted_iota(jnp.int32, st.shape, 0)
        t_col = s0 + (lax.broadcasted_iota(jnp.int32, st.shape, 1) & (tq - 1))
        valid = (c_idx * CMP_STRIDE + (CMP_BLOCK - 1) <= t_col) & (c_idx < nc)
        st = jnp.where(valid, st, NEG)
        e = jnp.where(valid, jnp.exp2(st - jnp.max(st, axis=0, keepdims=True)), 0.0)
        den = jnp.sum(e, axis=0, keepdims=True)
        pt = e / jnp.where(den > 0, den, 1.0)
        o_c.append(_dot(pt.T.astype(BF16), vc_ref[g]))
        psum = pt[:, 0:tq]
        for i in range(1, HPG):
            psum = psum + pt[:, i * tq:(i + 1) * tq]
        psums.append(psum)

    psum = jnp.concatenate(psums, axis=1)
    imp = sum(_dot(ov_ref[...], part) for part in _split3(psum))[0:ns, :]
    n_i = lax.broadcasted_iota(jnp.int32, imp.shape, 0)
    tt = s0 + (lax.broadcasted_iota(jnp.int32, imp.shape, 1) & (tq - 1))
    cur = tt // SEL_BLOCK
    forced = (n_i == 0) | (n_i == cur) | (n_i == cur - 1)
    score = jnp.where(forced, jnp.inf, jnp.where(n_i * SEL_BLOCK <= tt, imp, -jnp.inf))
    sel = jnp.zeros(imp.shape, jnp.bool_)
    for _ in range(ktop):
        mx = jnp.max(score, axis=0, keepdims=True)
        idx = jnp.min(jnp.where(score == mx, n_i, ns), axis=0, keepdims=True)
        hit = n_i == idx
        sel = sel | hit
        score = jnp.where(hit, -jnp.inf, score)
    bias_t = jnp.where(sel, 0.0, SEL_BIAS)
    bias_t = jnp.concatenate([bias_t, jnp.zeros((SLOT - ns, g_ * tq), F32)], axis=0)

    qr, qa = [], []
    for g in range(g_):
        bias = bias_t[:, g * tq:(g + 1) * tq].T.astype(BF16)
        heads = stack(qr_ref, g)
        qr.append(jnp.concatenate(heads, axis=0))
        qa.append(jnp.concatenate([q + bias for q in heads], axis=0))

    nchunk = (qi + 2) // 2
    dif = lax.broadcasted_iota(jnp.int32, (rows, kch), 1) - r_row

    def score_chunk(j, causal):
        k0 = pl.multiple_of(j * kch, kch)
        for g in range(g_):
            sc = _dot_nt(qa[g], ks_ref[pl.ds(k0, kch), g * SLOT:(g + 1) * SLOT])
            if causal:
                sc = jnp.where(dif <= s0 - k0, sc, NEG)
            s_ref[g, j] = sc
            mx_ref[g] = jnp.maximum(mx_ref[g], jnp.maximum(sc[:, :SLOT], sc[:, SLOT:]))

    mx_ref[...] = jnp.full(mx_ref.shape, NEG, F32)
    nfull = nchunk - 1

    def body1(j2, carry):
        score_chunk(2 * j2, False)
        score_chunk(2 * j2 + 1, False)
        return carry

    lax.fori_loop(0, nfull // 2, body1, 0)

    @pl.when(nfull % 2 == 1)
    def _():
        score_chunk(nfull - 1, False)

    score_chunk(nchunk - 1, True)
    for g in range(g_):
        mx_ref[g] = jnp.broadcast_to(jnp.max(mx_ref[g], axis=1, keepdims=True), (rows, SLOT))
    acc_ref[...] = jnp.zeros(acc_ref.shape, F32)

    def pv_chunks(j, nk):
        k0 = pl.multiple_of(j * kch, kch)
        ones_k = jnp.ones((nk * kch, SLOT), BF16)
        for g in range(g_):
            m = mx_ref[g]
            parts = []
            for i in range(nk):
                sc = s_ref[g, j + i]
                parts += [jnp.exp2(sc[:, :SLOT] - m), jnp.exp2(sc[:, SLOT:] - m)]
            p = jnp.concatenate(parts, axis=1).astype(BF16)
            v = jnp.concatenate([vs_ref[pl.ds(k0, nk * kch), g * SLOT:(g + 1) * SLOT], ones_k], axis=1)
            acc_ref[g] += _dot(p, v)

    def body2(j2, carry):
        pv_chunks(2 * j2, 2)
        return carry

    lax.fori_loop(0, nchunk // 2, body2, 0)

    @pl.when(nchunk % 2 == 1)
    def _():
        pv_chunks(nchunk - 1, 1)

    ones_w = jnp.ones((wlen, SLOT), BF16)
    nwt = wlen // tq
    wdif = lax.broadcasted_iota(jnp.int32, (rows, tq), 1) - r_row

    def window(interior):
        for g in range(g_):
            sw = _dot_nt(qr[g], kw_ref[pl.ds(wstart, wlen), g * SLOT:(g + 1) * SLOT])
            tiles = [sw[:, i * tq:(i + 1) * tq] for i in range(nwt)]
            if interior:
                tiles[0] = jnp.where(wdif > 0, tiles[0], NEG)
                tiles[-1] = jnp.where(wdif <= 0, tiles[-1], NEG)
            else:
                for i in range(nwt):
                    rel = wdif + (wstart + i * tq - s0)
                    tiles[i] = jnp.where((rel <= 0) & (rel > -WINDOW), tiles[i], NEG)
            mw = tiles[0]
            for t in tiles[1:]:
                mw = jnp.maximum(mw, t)
            mw = jnp.max(mw, axis=1, keepdims=True)
            pw = jnp.concatenate([jnp.exp2(t - mw) for t in tiles], axis=1).astype(BF16)
            vwin = jnp.concatenate([vw_ref[pl.ds(wstart, wlen), g * SLOT:(g + 1) * SLOT], ones_w], axis=1)
            accw = _dot(pw, vwin)
            ow_ref[g] = accw[:, :SLOT] / accw[:, SLOT:]

    @pl.when(s0 >= WINDOW)
    def _():
        window(True)

    @pl.when(s0 < WINDOW)
    def _():
        window(False)

    for g in range(g_):
        acc = acc_ref[g]
        o_s = acc[:, :SLOT] / acc[:, SLOT:]
        o_w = ow_ref[g]
        for hp in range(HPG // 2):
            col = (g * HPG + 2 * hp) * D_V
            ev, od = slice(2 * hp * tq, (2 * hp + 1) * tq), slice((2 * hp + 1) * tq, (2 * hp + 2) * tq)
            tile = jnp.zeros((tq, SLOT), F32)
            for br, o_b in enumerate((o_c[g], o_s, o_w)):
                pair = jnp.where(lane < D_V, o_b[ev], o_b[od])
                tile = tile + gx[:, br * hw + col:br * hw + col + SLOT] * pair
            o_ref[:, col:col + SLOT] = tile.astype(BF16)


def _nsa_attn(qn, qr, gates, kc, vc, ks, vs, kw, vw, *, nb, s):
    tq = Q_BLOCK
    nq = s // tq
    nc = (s - CMP_BLOCK) // CMP_STRIDE + 1
    ncp = s // CMP_STRIDE
    ns = s // SEL_BLOCK
    ktop = min(SEL_TOPK, ns)
    g_ = N_KV_GROUPS
    assert ncp % 8 == 0 and ns <= AUG and s >= WINDOW + tq and s % (2 * tq) == 0 and 2 * D_V == SLOT
    hw = N_HEADS * D_V
    cs = np.arange(ncp) * CMP_STRIDE
    ss = np.arange(SLOT) * SEL_BLOCK
    ov = ((cs[None, :] < ss[:, None] + SEL_BLOCK) & (cs[None, :] + CMP_BLOCK > ss[:, None])
          & (np.arange(ncp)[None, :] < nc) & (np.arange(SLOT)[:, None] < ns))
    ex = np.zeros((LANES, N_BRANCH * hw), np.float32)
    for hd in range(N_HEADS):
        for br in range(N_BRANCH):
            ex[hd * N_BRANCH + br, br * hw + hd * D_V:br * hw + (hd + 1) * D_V] = 1.0
    qspec = lambda wd: pl.BlockSpec((tq, wd), lambda b, q: (b * nq + q, 0))
    seq = lambda wd: pl.BlockSpec((s, wd), lambda b, q: (b, 0))
    cmp_spec = pl.BlockSpec((None, g_, ncp, SLOT), lambda b, q: (b, 0, 0, 0))
    nkc = s // (2 * tq)
    return pl.pallas_call(
        functools.partial(_nsa_kernel, nc=nc, ns=ns, ktop=ktop),
        grid=(nb, nq),
        in_specs=[qspec(N_HEADS * SLOT), qspec(N_HEADS * SLOT), qspec(LANES), cmp_spec, cmp_spec,
                  seq(g_ * SLOT), seq(g_ * SLOT), seq(g_ * SLOT), seq(g_ * SLOT),
                  _const_spec(ov.shape), _const_spec(ex.shape)],
        out_specs=qspec(hw),
        out_shape=jax.ShapeDtypeStruct((nb * s, hw), BF16),
        scratch_shapes=[pltpu.VMEM((g_, nkc, HPG * tq, 2 * tq), F32),
                        pltpu.VMEM((g_, HPG * tq, SLOT), F32),
                        pltpu.VMEM((g_, HPG * tq, 2 * SLOT), F32),
                        pltpu.VMEM((g_, HPG * tq, SLOT), F32)],
        compiler_params=_cparams(("parallel", "arbitrary")), name="nsa_attn",
    )(qn, qr, gates, kc.reshape(nb, g_, ncp, SLOT), vc.reshape(nb, g_, ncp, SLOT), ks, vs, kw, vw,
      jnp.asarray(ov, BF16), jnp.asarray(ex, BF16))


def kernel(x, norm_g, ffn_w_gu, ffn_w_down, a_w_in, a_conv_w, a_conv_b, a_w_ra, a_b_ra, a_w_ix, a_b_ix,
           a_lambda, a_w_out, kv_norm_g, kv_w, cmp_pos_k, cmp_w1_k, cmp_w2_k, cmp_pos_v, cmp_w1_v,
           cmp_w2_v, b_w_q, b_gate_bias, b_w_o, final_g):
    nb, s, d = x.shape
    n_a, n_b = a_w_in.shape[0], b_w_q.shape[0]
    depth = n_a + n_b
    assert n_a >= 1 and norm_g.shape[0] == depth

    h = x
    for layer in range(n_a):
        hm = _rglru_mixer(h, norm_g[layer, 0], a_w_in[layer], a_conv_w[layer], a_conv_b[layer],
                          a_w_ra[layer], a_b_ra[layer], a_w_ix[layer], a_b_ix[layer], a_lambda[layer],
                          a_w_out[layer])
        h = _ffn(hm.reshape(nb * s, d), norm_g[layer, 1], ffn_w_gu[layer], ffn_w_down[layer],
                 final_g=final_g if layer == depth - 1 else None).reshape(nb, s, d)
    if n_b == 0:
        return h
    h = h.reshape(nb * s, d)

    tables = _rope_tables(s)
    xk, xv, ks, vs, kw, vw = _kv_proj(h, kv_norm_g, kv_w, tables, nb=nb, s=s)
    nchunk = s // CMP_STRIDE
    w2k = jnp.pad(cmp_w2_k, ((0, 0), (AUG, 0)))
    w2v = jnp.tile(cmp_w2_v, (1, SLOT // D_V))
    kc = _compress(xk.reshape(-1, xk.shape[-1]), cmp_pos_k, cmp_w1_k, w2k, nchunk=nchunk)
    vc = _compress(xv.reshape(-1, xv.shape[-1]), cmp_pos_v, cmp_w1_v, w2v, nchunk=nchunk)

    for j in range(n_b):
        layer = n_a + j
        qn, qr, gates = _q_proj(h, norm_g[layer, 0], b_w_q[j], b_gate_bias[j], tables, s=s)
        attn = _nsa_attn(qn, qr, gates, kc, vc, ks, vs, kw, vw, nb=nb, s=s)
        h = _ffn(h, norm_g[layer, 1], ffn_w_gu[layer], ffn_w_down[layer], attn=attn, w_o=b_w_o[j],
                 final_g=final_g if layer == depth - 1 else None)
    return h.reshape(nb, s, d)
```

```python
import functools
import math

import numpy as np
import jax
import jax.numpy as jnp
from jax import lax
from jax.experimental import pallas as pl
from jax.experimental.pallas import tpu as pltpu

CONV_WIDTH = 4
RGLRU_C = 8.0
N_HEADS = 16
N_KV_GROUPS = 4
HPG = N_HEADS // N_KV_GROUPS
D_QK = 96
D_V = 64
ROPE_DIMS = D_QK // 4
ROPE_HALF = ROPE_DIMS // 2
ROPE_THETA = 500000.0
CMP_BLOCK = 32
CMP_STRIDE = 16
SEL_BLOCK = 64
SEL_TOPK = 8
WINDOW = 512
Q_BLOCK = 128
N_BRANCH = 3
EPS = 1e-6

LANES = 128
MXU_TILE = 256
SLOT = 128
AUG = SLOT - D_QK
VMEM_LIMIT = 56 * 1024 * 1024
NEG = -1e30
SEL_BIAS = -1e9
LOG2E = math.log2(math.e)

F32 = jnp.float32
BF16 = jnp.bfloat16


def _cparams(sem):
    return pltpu.CompilerParams(dimension_semantics=sem, vmem_limit_bytes=VMEM_LIMIT)


def _const_spec(shape):
    nd = len(shape)
    return pl.BlockSpec(shape, lambda *_: (0,) * nd, pipeline_mode=pl.Buffered(1))


def _rms(x, g):
    return x * lax.rsqrt(jnp.mean(x * x, axis=-1, keepdims=True) + EPS) * g


def _gelu(x):
    k1 = -2.0 * math.sqrt(2.0 / math.pi) * LOG2E
    return x / (1.0 + jnp.exp2(x * (k1 + (k1 * 0.044715) * (x * x))))


def _dot(a, b):
    return jnp.dot(a, b, preferred_element_type=F32)


def _dot_nt(a, b):
    return lax.dot_general(a, b, (((1,), (1,)), ((), ())), preferred_element_type=F32)


def _rglru_kernel(x_ref, g_ref, perm_ref, permt_ref, win_ref, cw_ref, cb_ref, wg_ref, bra_ref, bix_ref,
                  lam_ref, wout_ref, o_ref, xbe_ref, h_ref, ry_ref, *, ts, nb, cw, nch):
    rows = ts * nb
    halo = (CONV_WIDTH - 1) * nb
    d = x_ref.shape[-1]

    @pl.when(pl.program_id(0) == 0)
    def _():
        xbe_ref[:, 0:halo, :] = jnp.zeros((nch, halo, cw), F32)
        h_ref[...] = jnp.zeros_like(h_ref)

    hn_bt = _rms(x_ref[...].reshape(rows, d), g_ref[...]).astype(BF16)
    hn = _dot(perm_ref[...], hn_bt).astype(BF16)
    for c in range(nch):
        u = _dot(hn, win_ref[c])
        xbe_ref[c, halo:halo + rows, :] = u[:, :cw]
        y = _gelu(u[:, cw:])
        xc = cb_ref[c]
        for j in range(CONV_WIDTH):
            xc = xc + cw_ref[c, j:j + 1, :] * xbe_ref[c, j * nb:j * nb + rows, :]
        xbe_ref[c, 0:halo, :] = xbe_ref[c, rows:rows + halo, :]
        gt = _dot(xc.astype(BF16), wg_ref[c])
        r = jax.nn.sigmoid(gt[:, :cw] + bra_ref[c])
        ig = jax.nn.sigmoid(gt[:, cw:] + bix_ref[c])
        nla = r * (RGLRU_C * jax.nn.softplus(-lam_ref[c]))
        a = jnp.exp2(nla * (-LOG2E))
        om = jnp.tanh(nla) * (1.0 + a * a)
        bx = jnp.where(om > 0, om * lax.rsqrt(om), 0.0) * (ig * xc)
        h = h_ref[c]
        for t in range(ts):
            sl = slice(t * nb, (t + 1) * nb)
            h = a[sl] * h + bx[sl]
            ry_ref[sl, c * cw:(c + 1) * cw] = (h * y[sl]).astype(BF16)
        h_ref[c] = h
    ry_bt = _dot(permt_ref[...], ry_ref[...]).astype(BF16)
    out = _dot(ry_bt, wout_ref[...]) + x_ref[...].reshape(rows, d)
    o_ref[...] = out.reshape(nb, ts, d)


def _rglru_mixer(x, g, w_in, conv_w, conv_b, w_ra, b_ra, w_ix, b_ix, lam, w_out):
    nb, s, _ = x.shape
    d = w_in.shape[0]
    r = w_in.shape[1] // 2
    nblk, rb = w_ra.shape[0], w_ra.shape[1]
    per = LANES // math.gcd(rb, LANES)
    cw = per * rb
    nch = nblk // per
    assert nch * per == nblk and nch * cw == r
    ts = 16
    assert s % ts == 0 and nb % 16 == 0
    rows = ts * nb

    def chunks(v):
        return v.reshape(v.shape[0], nch, cw).transpose(1, 0, 2)

    def blockdiag(w):
        w = w.reshape(nch, per, rb, rb)
        return jnp.einsum('cpij,pq->cpiqj', w, jnp.eye(per, dtype=w.dtype)).reshape(nch, cw, cw)

    win = jnp.concatenate([chunks(w_in[:, :r]), chunks(w_in[:, r:])], axis=-1).astype(BF16)
    wg = jnp.concatenate([blockdiag(w_ra), blockdiag(w_ix)], axis=-1).astype(BF16)
    src = (np.arange(rows) % nb) * ts + np.arange(rows) // nb
    perm = np.zeros((rows, rows), np.float32)
    perm[np.arange(rows), src] = 1.0
    args = (x, g.reshape(1, d), jnp.asarray(perm, BF16), jnp.asarray(perm.T, BF16), win,
            chunks(conv_w), chunks(conv_b.reshape(1, r)), wg,
            chunks(b_ra.reshape(1, r)), chunks(b_ix.reshape(1, r)), chunks(lam.reshape(1, r)),
            w_out.astype(BF16))
    blk = pl.BlockSpec((nb, ts, d), lambda i: (0, i, 0))
    in_specs = [blk] + [_const_spec(a.shape) for a in args[1:]]
    kern = functools.partial(_rglru_kernel, ts=ts, nb=nb, cw=cw, nch=nch)
    return pl.pallas_call(
        kern,
        grid=(s // ts,),
        in_specs=in_specs,
        out_specs=blk,
        out_shape=jax.ShapeDtypeStruct((nb, s, d), F32),
        scratch_shapes=[
            pltpu.VMEM((nch, rows + (CONV_WIDTH - 1) * nb, cw), F32),
            pltpu.VMEM((nch, nb, cw), F32),
            pltpu.VMEM((rows, r), BF16),
        ],
        compiler_params=_cparams(("arbitrary",)),
        name="rglru_mixer",
    )(*args)


def _ffn_kernel(*refs, f, bounds, pre_attn, final_norm):
    it = iter(refs)
    h_ref = next(it)
    if pre_attn:
        attn_ref, wo_ref = next(it), next(it)
    g_ref, wgu_ref, wdn_ref = next(it), next(it), next(it)
    if final_norm:
        gf_ref = next(it)
    o_ref = next(it)

    h = h_ref[...]
    if pre_attn:
        h = h + _dot(attn_ref[...], wo_ref[...])
    hn = _rms(h, g_ref[...]).astype(BF16)
    acc = h
    for lo, hi in zip(bounds[:-1], bounds[1:]):
        gate = _dot(hn, wgu_ref[:, lo:hi])
        val = _dot(hn, wgu_ref[:, f + lo:f + hi])
        act = (jax.nn.silu(gate) * val).astype(BF16)
        acc = acc + _dot(act, wdn_ref[lo:hi, :])
    if final_norm:
        acc = _rms(acc, gf_ref[...])
    o_ref[...] = acc


def _ffn(h, g, w_gu, w_down, *, attn=None, w_o=None, final_g=None):
    n, d = h.shape
    f = w_down.shape[0]
    tr = 512
    mid = -(-f // (2 * MXU_TILE)) * MXU_TILE
    bounds = (0, mid, f) if mid < f else (0, f)
    assert n % tr == 0 and f % LANES == 0
    pre_attn = attn is not None
    final_norm = final_g is not None
    row_spec = lambda w: pl.BlockSpec((tr, w), lambda i: (i, 0))
    args, specs = [h], [row_spec(d)]
    if pre_attn:
        args += [attn, w_o.astype(BF16)]
        specs += [row_spec(attn.shape[1]), _const_spec(w_o.shape)]
    args += [g.reshape(1, d), w_gu.astype(BF16), w_down.astype(BF16)]
    specs += [_const_spec((1, d)), _const_spec(w_gu.shape), _const_spec(w_down.shape)]
    if final_norm:
        args.append(final_g.reshape(1, d))
        specs.append(_const_spec((1, d)))
    kern = functools.partial(_ffn_kernel, f=f, bounds=bounds, pre_attn=pre_attn, final_norm=final_norm)
    return pl.pallas_call(
        kern, grid=(n // tr,), in_specs=specs, out_specs=row_spec(d),
        out_shape=jax.ShapeDtypeStruct((n, d), F32),
        compiler_params=_cparams(("parallel",)), name="ffn",
    )(*args)


def _rope_slot(x, cos, sin_lo, sin_hi):
    return (x * cos + pltpu.roll(x, SLOT - ROPE_HALF, 1) * sin_lo
            + pltpu.roll(x, ROPE_HALF, 1) * sin_hi)


def _kv_proj_kernel(h_ref, g_ref, w_ref, cos_ref, slo_ref, shi_ref,
                    xk_ref, xv_ref, ks_ref, vs_ref, kw_ref, vw_ref, sc_ref, *, tr, s):
    g_ = N_KV_GROUPS
    hn = _rms(h_ref[...], g_ref[...]).astype(BF16)
    u = _dot(hn, w_ref[...])
    slot = lambda i: u[:, i * SLOT:(i + 1) * SLOT]
    cos, slo, shi = cos_ref[...], slo_ref[...], shi_ref[...]
    pos0 = (pl.program_id(0) % (s // tr)) * tr
    pos = pos0 + lax.broadcasted_iota(jnp.int32, (tr, SLOT), 0)
    lane = lax.broadcasted_iota(jnp.int32, (tr, SLOT), 1)
    onehot = (lane == pos // SEL_BLOCK).astype(F32)

    nj = tr // CMP_STRIDE
    for i in range(2 * g_):
        sc_ref[i] = slot(i)
    for x_ref, base in ((xk_ref, 0), (xv_ref, g_)):
        for gi in range(g_):
            for l in range(CMP_STRIDE):
                x_ref[gi, :, l * SLOT:(l + 1) * SLOT] = sc_ref[base + gi, pl.ds(l, nj, stride=CMP_STRIDE), :]

    vbase = 4 * g_
    for k_ref, v_ref, base, aug in ((ks_ref, vs_ref, 2 * g_, True), (kw_ref, vw_ref, 3 * g_, False)):
        for gi in range(g_):
            k = _rope_slot(slot(base + gi), cos, slo, shi)
            if aug:
                k = k + onehot
            k_ref[:, gi * SLOT:(gi + 1) * SLOT] = k.astype(BF16)
        for pair in range(g_ // 2):
            x = slot(vbase + pair)
            y = pltpu.roll(x, D_V, 1)
            v_ref[:, 2 * pair * SLOT:(2 * pair + 1) * SLOT] = jnp.where(lane < D_V, x, y).astype(BF16)
            v_ref[:, (2 * pair + 1) * SLOT:(2 * pair + 2) * SLOT] = jnp.where(lane < D_V, y, x).astype(BF16)
        vbase += g_ // 2


def _q_proj_kernel(h_ref, g_ref, w_ref, gb_ref, cos_ref, slo_ref, shi_ref, qn_ref, qr_ref, gt_ref):
    hn = _rms(h_ref[...], g_ref[...]).astype(BF16)
    u = _dot(hn, w_ref[...])
    cos, slo, shi = cos_ref[...], slo_ref[...], shi_ref[...]
    scale = D_QK ** -0.5 * LOG2E
    for hd in range(N_HEADS):
        q = u[:, hd * SLOT:(hd + 1) * SLOT] * scale
        qn_ref[:, hd * SLOT:(hd + 1) * SLOT] = q.astype(BF16)
        qr_ref[:, hd * SLOT:(hd + 1) * SLOT] = _rope_slot(q, cos, slo, shi).astype(BF16)
    gt_ref[...] = jax.nn.sigmoid(u[:, N_HEADS * SLOT:] + gb_ref[...])


def _pad_slots(w, n, width, left):
    d = w.shape[0]
    w = w.reshape(d, n, width)
    return jnp.pad(w, ((0, 0), (0, 0), (left, SLOT - width - left))).reshape(d, n * SLOT)


def _rope_tables(s):
    inv = ROPE_THETA ** (-jnp.arange(ROPE_HALF, dtype=F32) * 2.0 / ROPE_DIMS)
    ang = jnp.arange(s, dtype=F32)[:, None] * inv[None, :]
    cos, sin = jnp.cos(ang), jnp.sin(ang)
    z = lambda n: jnp.zeros((s, n), F32)
    o = lambda n: jnp.ones((s, n), F32)
    rest = SLOT - AUG - ROPE_DIMS
    cos_t = jnp.concatenate([o(AUG), cos, cos, o(rest)], axis=1)
    sin_lo = jnp.concatenate([z(AUG), -sin, z(ROPE_HALF), z(rest)], axis=1)
    sin_hi = jnp.concatenate([z(AUG), z(ROPE_HALF), sin, z(rest)], axis=1)
    return cos_t, sin_lo, sin_hi


def _kv_proj(h, g, kv_w, tables, *, nb, s):
    n, d = h.shape
    g_, dk, dv = N_KV_GROUPS, D_QK, D_V
    tr = 512
    assert n % tr == 0 and s % tr == 0 and tr % CMP_STRIDE == 0
    sizes = [g_ * dk, g_ * dv] * 3
    offs = np.concatenate([[0], np.cumsum(sizes)])
    kc, vc, ksl, vsl, kwn, vwn = [kv_w[:, offs[j]:offs[j + 1]] for j in range(6)]
    assert 2 * dv == SLOT and g_ % 2 == 0
    w = jnp.concatenate([_pad_slots(kc, g_, dk, 0), _pad_slots(vc, g_, dv, 0),
                         _pad_slots(ksl, g_, dk, AUG), _pad_slots(kwn, g_, dk, AUG), vsl, vwn],
                        axis=1).astype(BF16)
    row = lambda wd, dt: (pl.BlockSpec((tr, wd), lambda i: (i, 0)), jax.ShapeDtypeStruct((n, wd), dt))
    nj = tr // CMP_STRIDE
    spt = s // tr
    xcmp = (pl.BlockSpec((None, g_, nj, CMP_STRIDE * SLOT), lambda i: (i // spt, 0, i % spt, 0)),
            jax.ShapeDtypeStruct((nb, g_, s // CMP_STRIDE, CMP_STRIDE * SLOT), F32))
    outs = [xcmp, xcmp, row(g_ * SLOT, BF16), row(g_ * SLOT, BF16), row(g_ * SLOT, BF16), row(g_ * SLOT, BF16)]
    tab_spec = pl.BlockSpec((tr, SLOT), lambda i: (i % spt, 0))
    return pl.pallas_call(
        functools.partial(_kv_proj_kernel, tr=tr, s=s),
        grid=(n // tr,),
        in_specs=[pl.BlockSpec((tr, d), lambda i: (i, 0)), _const_spec((1, d)), _const_spec(w.shape),
                  tab_spec, tab_spec, tab_spec],
        out_specs=[o[0] for o in outs], out_shape=[o[1] for o in outs],
        scratch_shapes=[pltpu.VMEM((2 * g_, tr, SLOT), F32)],
        compiler_params=_cparams(("parallel",)), name="kv_proj",
    )(h, g.reshape(1, d), w, *tables)


def _q_proj(h, g, w_q, gate_bias, tables, *, s):
    n, d = h.shape
    tr = 512
    nq = N_HEADS * D_QK
    ngate = N_HEADS * N_BRANCH
    assert ngate <= LANES
    w = jnp.concatenate([_pad_slots(w_q[:, :nq], N_HEADS, D_QK, AUG),
                         jnp.pad(w_q[:, nq:], ((0, 0), (0, LANES - ngate)))], axis=1).astype(BF16)
    gb = jnp.pad(gate_bias.reshape(1, ngate), ((0, 0), (0, LANES - ngate)))
    tab_spec = pl.BlockSpec((tr, SLOT), lambda i: (i % (s // tr), 0))
    row = lambda wd, dt: (pl.BlockSpec((tr, wd), lambda i: (i, 0)), jax.ShapeDtypeStruct((n, wd), dt))
    outs = [row(N_HEADS * SLOT, BF16), row(N_HEADS * SLOT, BF16), row(LANES, F32)]
    return pl.pallas_call(
        _q_proj_kernel,
        grid=(n // tr,),
        in_specs=[pl.BlockSpec((tr, d), lambda i: (i, 0)), _const_spec((1, d)), _const_spec(w.shape),
                  _const_spec((1, LANES)), tab_spec, tab_spec, tab_spec],
        out_specs=[o[0] for o in outs], out_shape=[o[1] for o in outs],
        compiler_params=_cparams(("parallel",)), name="q_proj",
    )(h, g.reshape(1, d), w, gb, *tables)


def _compress_kernel(x_ref, plo_ref, phi_ref, w1lo_ref, w1hi_ref, w2_ref, o_ref):
    x = x_ref[...]
    lo = _dot((x + plo_ref[...]).astype(BF16), w1lo_ref[...])
    hi = _dot((x + phi_ref[...]).astype(BF16), w1hi_ref[...])
    pre = lo + pltpu.roll(hi, hi.shape[0] - 1, 0)
    o_ref[...] = _dot(_gelu(pre).astype(BF16), w2_ref[...]).astype(BF16)


def _compress(x, pos, w1, w2_slot, *, nchunk):
    n, width = x.shape
    dh = pos.shape[1]
    hid = w1.shape[1]
    assert CMP_BLOCK == 2 * CMP_STRIDE and width == CMP_STRIDE * SLOT
    tr = 4 * nchunk
    assert n % tr == 0

    def slots(v):
        k = v.shape[1]
        v = jnp.pad(v.reshape(2, CMP_STRIDE, dh, k), ((0, 0), (0, 0), (0, SLOT - dh), (0, 0)))
        return v.reshape(2, width, k)

    p = slots(pos.reshape(CMP_BLOCK * dh, 1))[:, :, 0].reshape(2, 1, width)
    w1s = slots(w1).astype(BF16)
    return pl.pallas_call(
        _compress_kernel,
        grid=(n // tr,),
        in_specs=[pl.BlockSpec((tr, width), lambda i: (i, 0)), _const_spec((1, width)), _const_spec((1, width)),
                  _const_spec((width, hid)), _const_spec((width, hid)), _const_spec((hid, SLOT))],
        out_specs=pl.BlockSpec((tr, SLOT), lambda i: (i, 0)),
        out_shape=jax.ShapeDtypeStruct((n, SLOT), BF16),
        compiler_params=_cparams(("parallel",)), name="compress",
    )(x, p[0], p[1], w1s[0], w1s[1], w2_slot.astype(BF16))


def _split3(x):
    hi = x.astype(BF16)
    r1 = x - hi.astype(F32)
    mid = r1.astype(BF16)
    lo = (r1 - mid.astype(F32)).astype(BF16)
    return hi, mid, lo


def _nsa_kernel(qn_ref, qr_ref, gt_ref, kc_ref, vc_ref, ks_ref, vs_ref, kw_ref, vw_ref, ov_ref, ex_ref,
                o_ref, s_ref, mx_ref, acc_ref, kwp_ref, vwp_ref, *, nc, ns, ktop):
    g_ = N_KV_GROUPS
    tq = Q_BLOCK
    rows = HPG * tq
    kch = 2 * tq
    wlen = WINDOW + tq
    qi = pl.program_id(1)
    s0 = pl.multiple_of(qi * tq, tq)
    hw = N_HEADS * D_V
    lane = lax.broadcasted_iota(jnp.int32, (tq, SLOT), 1)

    @pl.when(qi == 0)
    def _():
        pad_lane = lax.broadcasted_iota(jnp.int32, (WINDOW, g_ * SLOT), 1) & (SLOT - 1)
        kwp_ref[0:WINDOW, :] = (pad_lane == 0).astype(BF16)
        vwp_ref[0:WINDOW, :] = jnp.zeros((WINDOW, g_ * SLOT), BF16)
        kwp_ref[WINDOW:, :] = kw_ref[...]
        vwp_ref[WINDOW:, :] = vw_ref[...]

    ghi, gmid, _ = _split3(gt_ref[...])
    gx = _dot(ghi, ex_ref[...]) + _dot(gmid, ex_ref[...])

    r_row = lax.broadcasted_iota(jnp.int32, (rows, 1), 0) & (tq - 1)
    stack = lambda ref, g: [ref[:, hd * SLOT:(hd + 1) * SLOT] for hd in range(g * HPG, (g + 1) * HPG)]

    o_c, psums = [], []
    for g in range(g_):
        st = _dot_nt(kc_ref[g], jnp.concatenate(stack(qn_ref, g), axis=0))
        c_idx = lax.broadcasted_iota(jnp.int32, st.shape, 0)
        t_col = s0 + (lax.broadcasted_iota(jnp.int32, st.shape, 1) & (tq - 1))
        valid = (c_idx * CMP_STRIDE + (CMP_BLOCK - 1) <= t_col) & (c_idx < nc)
        st = jnp.where(valid, st, NEG)
        e = jnp.where(valid, jnp.exp2(st - jnp.max(st, axis=0, keepdims=True)), 0.0)
        den = jnp.sum(e, axis=0, keepdims=True)
        pt = e / jnp.where(den > 0, den, 1.0)
        o_c.append(_dot(pt.astype(BF16).T, vc_ref[g]))
        psum = pt[:, 0:tq]
        for i in range(1, HPG):
            psum = psum + pt[:, i * tq:(i + 1) * tq]
        psums.append(psum)

    psum = jnp.concatenate(psums, axis=1)
    imp = sum(_dot(ov_ref[...], part) for part in _split3(psum))[0:ns, :]
    n_i = lax.broadcasted_iota(jnp.int32, imp.shape, 0)
    tt = s0 + (lax.broadcasted_iota(jnp.int32, imp.shape, 1) & (tq - 1))
    cur = tt // SEL_BLOCK
    forced = (n_i == 0) | (n_i == cur) | (n_i == cur - 1)
    score = jnp.where(forced, jnp.inf, jnp.where(n_i * SEL_BLOCK <= tt, imp, -jnp.inf))
    sel = jnp.zeros(imp.shape, jnp.bool_)
    for _ in range(ktop):
        mx = jnp.max(score, axis=0, keepdims=True)
        idx = jnp.min(jnp.where(score == mx, n_i, ns), axis=0, keepdims=True)
        hit = n_i == idx
        sel = sel | hit
        score = jnp.where(hit, -jnp.inf, score)
    bias_t = jnp.where(sel, 0.0, SEL_BIAS).astype(BF16)
    bias_t = jnp.concatenate([bias_t, jnp.zeros((SLOT - ns, g_ * tq), BF16)], axis=0)

    qa = []
    for g in range(g_):
        bias = bias_t[:, g * tq:(g + 1) * tq].T
        qa.append(jnp.concatenate([q + bias for q in stack(qr_ref, g)], axis=0))

    nchunk = (qi + 2) // 2
    dif = lax.broadcasted_iota(jnp.int32, (rows, kch), 1) - r_row

    def score_chunk(j, causal):
        k0 = pl.multiple_of(j * kch, kch)
        for g in range(g_):
            sc = _dot_nt(qa[g], ks_ref[pl.ds(k0, kch), g * SLOT:(g + 1) * SLOT])
            if causal:
                sc = jnp.where(dif <= s0 - k0, sc, NEG)
            s_ref[g, j] = sc
            mx_ref[g] = jnp.maximum(mx_ref[g], jnp.maximum(sc[:, :SLOT], sc[:, SLOT:]))

    mx_ref[...] = jnp.full(mx_ref.shape, NEG, F32)
    nfull = nchunk - 1

    def body1(j2, carry):
        score_chunk(2 * j2, False)
        score_chunk(2 * j2 + 1, False)
        return carry

    lax.fori_loop(0, nfull // 2, body1, 0)

    @pl.when(nfull % 2 == 1)
    def _():
        score_chunk(nfull - 1, False)

    score_chunk(nchunk - 1, True)
    for g in range(g_):
        mx_ref[g] = jnp.broadcast_to(jnp.max(mx_ref[g], axis=1, keepdims=True), (rows, SLOT))
    acc_ref[...] = jnp.zeros(acc_ref.shape, F32)

    def pv_chunks(j, nk):
        k0 = pl.multiple_of(j * kch, kch)
        ones_k = jnp.ones((nk * kch, SLOT), BF16)
        for g in range(g_):
            m = mx_ref[g]
            parts = []
            for i in range(nk):
                sc = s_ref[g, j + i]
                parts += [jnp.exp2(sc[:, :SLOT] - m), jnp.exp2(sc[:, SLOT:] - m)]
            p = jnp.concatenate(parts, axis=1).astype(BF16)
            v = jnp.concatenate([vs_ref[pl.ds(k0, nk * kch), g * SLOT:(g + 1) * SLOT], ones_k], axis=1)
            acc_ref[g] += _dot(p, v)

    def body2(j2, carry):
        pv_chunks(2 * j2, 2)
        return carry

    lax.fori_loop(0, nchunk // 2, body2, 0)

    @pl.when(nchunk % 2 == 1)
    def _():
        pv_chunks(nchunk - 1, 1)

    ones_w = jnp.ones((wlen, SLOT), BF16)
    nwt = wlen // tq
    wdif = lax.broadcasted_iota(jnp.int32, (rows, tq), 1) - r_row
    pad_bias = jnp.where(lane == 0, SEL_BIAS, 0.0).astype(BF16)
    o_ws = []
    for g in range(g_):
        qw = jnp.concatenate([q + pad_bias for q in stack(qr_ref, g)], axis=0)
        sw = _dot_nt(qw, kwp_ref[pl.ds(s0, wlen), g * SLOT:(g + 1) * SLOT])
        tiles = [sw[:, i * tq:(i + 1) * tq] for i in range(nwt)]
        tiles[0] = jnp.where(wdif > 0, tiles[0], NEG)
        tiles[-1] = jnp.where(wdif <= 0, tiles[-1], NEG)
        mw = tiles[0]
        for t in tiles[1:]:
            mw = jnp.maximum(mw, t)
        mw = jnp.max(mw, axis=1, keepdims=True)
        pw = jnp.concatenate([jnp.exp2(t - mw) for t in tiles], axis=1).astype(BF16)
        vwin = jnp.concatenate([vwp_ref[pl.ds(s0, wlen), g * SLOT:(g + 1) * SLOT], ones_w], axis=1)
        accw = _dot(pw, vwin)
        o_ws.append(accw[:, :SLOT] / accw[:, SLOT:])

    for g in range(g_):
        acc = acc_ref[g]
        o_s = acc[:, :SLOT] / acc[:, SLOT:]
        o_w = o_ws[g]
        for hp in range(HPG // 2):
            col = (g * HPG + 2 * hp) * D_V
            ev, od = slice(2 * hp * tq, (2 * hp + 1) * tq), slice((2 * hp + 1) * tq, (2 * hp + 2) * tq)
            tile = jnp.zeros((tq, SLOT), F32)
            for br, o_b in enumerate((o_c[g], o_s, o_w)):
                pair = jnp.where(lane < D_V, o_b[ev], o_b[od])
                tile = tile + gx[:, br * hw + col:br * hw + col + SLOT] * pair
            o_ref[:, col:col + SLOT] = tile.astype(BF16)


def _nsa_attn(qn, qr, gates, kc, vc, ks, vs, kw, vw, *, nb, s):
    tq = Q_BLOCK
    nq = s // tq
    nc = (s - CMP_BLOCK) // CMP_STRIDE + 1
    ncp = s // CMP_STRIDE
    ns = s // SEL_BLOCK
    ktop = min(SEL_TOPK, ns)
    g_ = N_KV_GROUPS
    assert ncp % 8 == 0 and ns <= AUG and s >= WINDOW + tq and s % (2 * tq) == 0 and 2 * D_V == SLOT
    hw = N_HEADS * D_V
    cs = np.arange(ncp) * CMP_STRIDE
    ss = np.arange(SLOT) * SEL_BLOCK
    ov = ((cs[None, :] < ss[:, None] + SEL_BLOCK) & (cs[None, :] + CMP_BLOCK > ss[:, None])
          & (np.arange(ncp)[None, :] < nc) & (np.arange(SLOT)[:, None] < ns))
    ex = np.zeros((LANES, N_BRANCH * hw), np.float32)
    for hd in range(N_HEADS):
        for br in range(N_BRANCH):
            ex[hd * N_BRANCH + br, br * hw + hd * D_V:br * hw + (hd + 1) * D_V] = 1.0
    qspec = lambda wd: pl.BlockSpec((tq, wd), lambda b, q: (b * nq + q, 0))
    seq = lambda wd: pl.BlockSpec((s, wd), lambda b, q: (b, 0))
    cmp_spec = pl.BlockSpec((None, g_, ncp, SLOT), lambda b, q: (b, 0, 0, 0))
    nkc = s // (2 * tq)
    return pl.pallas_call(
        functools.partial(_nsa_kernel, nc=nc, ns=ns, ktop=ktop),
        grid=(nb, nq),
        in_specs=[qspec(N_HEADS * SLOT), qspec(N_HEADS * SLOT), qspec(LANES), cmp_spec, cmp_spec,
                  seq(g_ * SLOT), seq(g_ * SLOT), seq(g_ * SLOT), seq(g_ * SLOT),
                  _const_spec(ov.shape), _const_spec(ex.shape)],
        out_specs=qspec(hw),
        out_shape=jax.ShapeDtypeStruct((nb * s, hw), BF16),
        scratch_shapes=[pltpu.VMEM((g_, nkc, HPG * tq, 2 * tq), F32),
                        pltpu.VMEM((g_, HPG * tq, SLOT), F32),
                        pltpu.VMEM((g_, HPG * tq, 2 * SLOT), F32),
                        pltpu.VMEM((s + WINDOW, g_ * SLOT), BF16),
                        pltpu.VMEM((s + WINDOW, g_ * SLOT), BF16)],
        compiler_params=_cparams(("parallel", "arbitrary")), name="nsa_attn",
    )(qn, qr, gates, kc.reshape(nb, g_, ncp, SLOT), vc.reshape(nb, g_, ncp, SLOT), ks, vs, kw, vw,
      jnp.asarray(ov, BF16), jnp.asarray(ex, BF16))


def kernel(x, norm_g, ffn_w_gu, ffn_w_down, a_w_in, a_conv_w, a_conv_b, a_w_ra, a_b_ra, a_w_ix, a_b_ix,
           a_lambda, a_w_out, kv_norm_g, kv_w, cmp_pos_k, cmp_w1_k, cmp_w2_k, cmp_pos_v, cmp_w1_v,
           cmp_w2_v, b_w_q, b_gate_bias, b_w_o, final_g):
    nb, s, d = x.shape
    n_a, n_b = a_w_in.shape[0], b_w_q.shape[0]
    depth = n_a + n_b
    assert n_a >= 1 and norm_g.shape[0] == depth

    h = x
    for layer in range(n_a):
        hm = _rglru_mixer(h, norm_g[layer, 0], a_w_in[layer], a_conv_w[layer], a_conv_b[layer],
                          a_w_ra[layer], a_b_ra[layer], a_w_ix[layer], a_b_ix[layer], a_lambda[layer],
                          a_w_out[layer])
        h = _ffn(hm.reshape(nb * s, d), norm_g[layer, 1], ffn_w_gu[layer], ffn_w_down[layer],
                 final_g=final_g if layer == depth - 1 else None).reshape(nb, s, d)
    if n_b == 0:
        return h
    h = h.reshape(nb * s, d)

    tables = _rope_tables(s)
    xk, xv, ks, vs, kw, vw = _kv_proj(h, kv_norm_g, kv_w, tables, nb=nb, s=s)
    nchunk = s // CMP_STRIDE
    w2k = jnp.pad(cmp_w2_k, ((0, 0), (AUG, 0)))
    w2v = jnp.tile(cmp_w2_v, (1, SLOT // D_V))
    kc = _compress(xk.reshape(-1, xk.shape[-1]), cmp_pos_k, cmp_w1_k, w2k, nchunk=nchunk)
    vc = _compress(xv.reshape(-1, xv.shape[-1]), cmp_pos_v, cmp_w1_v, w2v, nchunk=nchunk)

    for j in range(n_b):
        layer = n_a + j
        qn, qr, gates = _q_proj(h, norm_g[layer, 0], b_w_q[j], b_gate_bias[j], tables, s=s)
        attn = _nsa_attn(qn, qr, gates, kc, vc, ks, vs, kw, vw, nb=nb, s=s)
        h = _ffn(h, norm_g[layer, 1], ffn_w_gu[layer], ffn_w_down[layer], attn=attn, w_o=b_w_o[j],
                 final_g=final_g if layer == depth - 1 else None)
    return h.reshape(nb, s, d)
```

```python
import functools
import math

import numpy as np
import jax
import jax.numpy as jnp
from jax import lax
from jax.experimental import pallas as pl
from jax.experimental.pallas import tpu as pltpu

CONV_WIDTH = 4
RGLRU_C = 8.0
N_HEADS = 16
N_KV_GROUPS = 4
HPG = N_HEADS // N_KV_GROUPS
D_QK = 96
D_V = 64
ROPE_DIMS = D_QK // 4
ROPE_HALF = ROPE_DIMS // 2
ROPE_THETA = 500000.0
CMP_BLOCK = 32
CMP_STRIDE = 16
SEL_BLOCK = 64
SEL_TOPK = 8
WINDOW = 512
Q_BLOCK = 128
N_BRANCH = 3
EPS = 1e-6

LANES = 128
MXU_TILE = 256
SLOT = 128
AUG = SLOT - D_QK
VMEM_LIMIT = 56 * 1024 * 1024
NEG = -1e30
SEL_BIAS = -1e9
LOG2E = math.log2(math.e)

F32 = jnp.float32
BF16 = jnp.bfloat16


def _cparams(sem):
    return pltpu.CompilerParams(dimension_semantics=sem, vmem_limit_bytes=VMEM_LIMIT)


def _const_spec(shape):
    nd = len(shape)
    return pl.BlockSpec(shape, lambda *_: (0,) * nd, pipeline_mode=pl.Buffered(1))


def _rms(x, g):
    return x * lax.rsqrt(jnp.mean(x * x, axis=-1, keepdims=True) + EPS) * g


def _gelu(x):
    k1 = -2.0 * math.sqrt(2.0 / math.pi) * LOG2E
    return x / (1.0 + jnp.exp2(x * (k1 + (k1 * 0.044715) * (x * x))))


def _dot(a, b):
    return jnp.dot(a, b, preferred_element_type=F32)


def _dot_nt(a, b):
    return lax.dot_general(a, b, (((1,), (1,)), ((), ())), preferred_element_type=F32)


def _rglru_kernel(x_hbm, g_ref, win_ref, cw_ref, cb_ref, wg_ref, bra_ref, bix_ref, lam_ref, wout_ref,
                  o_hbm, xbe_ref, h_ref, ry_ref, xt_ref, ot_ref, sem_in, sem_out, *, ts, nb, cw, nch, nsteps):
    rows = ts * nb
    halo = (CONV_WIDTH - 1) * nb
    d = xt_ref.shape[-1]
    i = pl.program_id(0)
    slot = i % 2

    def in_copies(step, sl):
        return [pltpu.make_async_copy(x_hbm.at[:, step * ts + t, :], xt_ref.at[sl, t], sem_in.at[sl])
                for t in range(ts)]

    def out_copies(step, sl):
        return [pltpu.make_async_copy(ot_ref.at[sl, t], o_hbm.at[:, step * ts + t, :], sem_out.at[sl])
                for t in range(ts)]

    @pl.when(i == 0)
    def _():
        xbe_ref[:, 0:halo, :] = jnp.zeros((nch, halo, cw), F32)
        h_ref[...] = jnp.zeros_like(h_ref)
        for cp in in_copies(0, 0):
            cp.start()

    @pl.when(i + 1 < nsteps)
    def _():
        for cp in in_copies(i + 1, 1 - slot):
            cp.start()

    for cp in in_copies(i, slot):
        cp.wait()

    hn = _rms(xt_ref[slot].reshape(rows, d), g_ref[...]).astype(BF16)
    for c in range(nch):
        u = _dot(hn, win_ref[c])
        xbe_ref[c, halo:halo + rows, :] = u[:, :cw]
        y = _gelu(u[:, cw:])
        xc = cb_ref[c]
        for j in range(CONV_WIDTH):
            xc = xc + cw_ref[c, j:j + 1, :] * xbe_ref[c, j * nb:j * nb + rows, :]
        xbe_ref[c, 0:halo, :] = xbe_ref[c, rows:rows + halo, :]
        gt = _dot(xc.astype(BF16), wg_ref[c])
        r = jax.nn.sigmoid(gt[:, :cw] + bra_ref[c])
        ig = jax.nn.sigmoid(gt[:, cw:] + bix_ref[c])
        nla = r * (RGLRU_C * jax.nn.softplus(-lam_ref[c]))
        a = jnp.exp2(nla * (-LOG2E))
        om = jnp.tanh(nla) * (1.0 + a * a)
        bx = jnp.where(om > 0, om * lax.rsqrt(om), 0.0) * (ig * xc)
        h = h_ref[c]
        for t in range(ts):
            sl = slice(t * nb, (t + 1) * nb)
            h = a[sl] * h + bx[sl]
            ry_ref[sl, c * cw:(c + 1) * cw] = (h * y[sl]).astype(BF16)
        h_ref[c] = h
    out = _dot(ry_ref[...], wout_ref[...]) + xt_ref[slot].reshape(rows, d)

    @pl.when(i >= 2)
    def _():
        for cp in out_copies(i - 2, slot):
            cp.wait()

    ot_ref[slot] = out.reshape(ts, nb, d)
    for cp in out_copies(i, slot):
        cp.start()

    @pl.when(i == nsteps - 1)
    def _():
        for cp in out_copies(i, slot):
            cp.wait()
        if nsteps > 1:
            for cp in out_copies(i - 1, 1 - slot):
                cp.wait()


def _rglru_mixer(x, g, w_in, conv_w, conv_b, w_ra, b_ra, w_ix, b_ix, lam, w_out):
    nb, s, _ = x.shape
    d = w_in.shape[0]
    r = w_in.shape[1] // 2
    nblk, rb = w_ra.shape[0], w_ra.shape[1]
    per = LANES // math.gcd(rb, LANES)
    cw = per * rb
    nch = nblk // per
    assert nch * per == nblk and nch * cw == r
    ts = 16
    assert s % ts == 0 and nb % 16 == 0
    rows = ts * nb

    def chunks(v):
        return v.reshape(v.shape[0], nch, cw).transpose(1, 0, 2)

    def blockdiag(w):
        w = w.reshape(nch, per, rb, rb)
        return jnp.einsum('cpij,pq->cpiqj', w, jnp.eye(per, dtype=w.dtype)).reshape(nch, cw, cw)

    win = jnp.concatenate([chunks(w_in[:, :r]), chunks(w_in[:, r:])], axis=-1).astype(BF16)
    wg = jnp.concatenate([blockdiag(w_ra), blockdiag(w_ix)], axis=-1).astype(BF16)
    args = (x, g.reshape(1, d), win, chunks(conv_w), chunks(conv_b.reshape(1, r)), wg,
            chunks(b_ra.reshape(1, r)), chunks(b_ix.reshape(1, r)), chunks(lam.reshape(1, r)),
            w_out.astype(BF16))
    hbm = pl.BlockSpec(memory_space=pl.ANY)
    in_specs = [hbm] + [_const_spec(a.shape) for a in args[1:]]
    kern = functools.partial(_rglru_kernel, ts=ts, nb=nb, cw=cw, nch=nch, nsteps=s // ts)
    return pl.pallas_call(
        kern,
        grid=(s // ts,),
        in_specs=in_specs,
        out_specs=hbm,
        out_shape=jax.ShapeDtypeStruct((nb, s, d), F32),
        scratch_shapes=[
            pltpu.VMEM((nch, rows + (CONV_WIDTH - 1) * nb, cw), F32),
            pltpu.VMEM((nch, nb, cw), F32),
            pltpu.VMEM((rows, r), BF16),
            pltpu.VMEM((2, ts, nb, d), F32),
            pltpu.VMEM((2, ts, nb, d), F32),
            pltpu.SemaphoreType.DMA((2,)),
            pltpu.SemaphoreType.DMA((2,)),
        ],
        compiler_params=_cparams(("arbitrary",)),
        name="rglru_mixer",
    )(*args)


def _ffn_kernel(*refs, f, bounds, pre_attn, final_norm):
    it = iter(refs)
    h_ref = next(it)
    if pre_attn:
        attn_ref, wo_ref = next(it), next(it)
    g_ref, wgu_ref, wdn_ref = next(it), next(it), next(it)
    if final_norm:
        gf_ref = next(it)
    o_ref = next(it)

    h = h_ref[...]
    if pre_attn:
        h = h + _dot(attn_ref[...], wo_ref[...])
    hn = _rms(h, g_ref[...]).astype(BF16)
    acc = h
    for lo, hi in zip(bounds[:-1], bounds[1:]):
        gate = _dot(hn, wgu_ref[:, lo:hi])
        val = _dot(hn, wgu_ref[:, f + lo:f + hi])
        act = (jax.nn.silu(gate) * val).astype(BF16)
        acc = acc + _dot(act, wdn_ref[lo:hi, :])
    if final_norm:
        acc = _rms(acc, gf_ref[...])
    o_ref[...] = acc


def _ffn(h, g, w_gu, w_down, *, attn=None, w_o=None, final_g=None):
    n, d = h.shape
    f = w_down.shape[0]
    tr = 512
    mid = -(-f // (2 * MXU_TILE)) * MXU_TILE
    bounds = (0, mid, f) if mid < f else (0, f)
    assert n % tr == 0 and f % LANES == 0
    pre_attn = attn is not None
    final_norm = final_g is not None
    row_spec = lambda w: pl.BlockSpec((tr, w), lambda i: (i, 0))
    args, specs = [h], [row_spec(d)]
    if pre_attn:
        args += [attn, w_o.astype(BF16)]
        specs += [row_spec(attn.shape[1]), _const_spec(w_o.shape)]
    args += [g.reshape(1, d), w_gu.astype(BF16), w_down.astype(BF16)]
    specs += [_const_spec((1, d)), _const_spec(w_gu.shape), _const_spec(w_down.shape)]
    if final_norm:
        args.append(final_g.reshape(1, d))
        specs.append(_const_spec((1, d)))
    kern = functools.partial(_ffn_kernel, f=f, bounds=bounds, pre_attn=pre_attn, final_norm=final_norm)
    return pl.pallas_call(
        kern, grid=(n // tr,), in_specs=specs, out_specs=row_spec(d),
        out_shape=jax.ShapeDtypeStruct((n, d), F32),
        compiler_params=_cparams(("parallel",)), name="ffn",
    )(*args)


def _rope_slot(x, cos, sin_lo, sin_hi):
    return (x * cos + pltpu.roll(x, SLOT - ROPE_HALF, 1) * sin_lo
            + pltpu.roll(x, ROPE_HALF, 1) * sin_hi)


def _kv_proj_kernel(h_ref, g_ref, w_ref, cos_ref, slo_ref, shi_ref,
                    xk_ref, xv_ref, ks_ref, vs_ref, kw_ref, vw_ref, sc_ref, *, tr, s):
    g_ = N_KV_GROUPS
    hn = _rms(h_ref[...], g_ref[...]).astype(BF16)
    u = _dot(hn, w_ref[...])
    slot = lambda i: u[:, i * SLOT:(i + 1) * SLOT]
    cos, slo, shi = cos_ref[...], slo_ref[...], shi_ref[...]
    pos0 = (pl.program_id(0) % (s // tr)) * tr
    pos = pos0 + lax.broadcasted_iota(jnp.int32, (tr, SLOT), 0)
    lane = lax.broadcasted_iota(jnp.int32, (tr, SLOT), 1)
    onehot = (lane == pos // SEL_BLOCK).astype(F32)

    nj = tr // CMP_STRIDE
    for i in range(2 * g_):
        sc_ref[i] = slot(i)
    for x_ref, base in ((xk_ref, 0), (xv_ref, g_)):
        for gi in range(g_):
            for l in range(CMP_STRIDE):
                x_ref[gi, :, l * SLOT:(l + 1) * SLOT] = sc_ref[base + gi, pl.ds(l, nj, stride=CMP_STRIDE), :]

    vbase = 4 * g_
    for k_ref, v_ref, base, aug in ((ks_ref, vs_ref, 2 * g_, True), (kw_ref, vw_ref, 3 * g_, False)):
        for gi in range(g_):
            k = _rope_slot(slot(base + gi), cos, slo, shi)
            if aug:
                k = k + onehot
            k_ref[:, gi * SLOT:(gi + 1) * SLOT] = k.astype(BF16)
        for pair in range(g_ // 2):
            x = slot(vbase + pair)
            y = pltpu.roll(x, D_V, 1)
            v_ref[:, 2 * pair * SLOT:(2 * pair + 1) * SLOT] = jnp.where(lane < D_V, x, y).astype(BF16)
            v_ref[:, (2 * pair + 1) * SLOT:(2 * pair + 2) * SLOT] = jnp.where(lane < D_V, y, x).astype(BF16)
        vbase += g_ // 2


def _q_proj_kernel(h_ref, g_ref, w_ref, gb_ref, cos_ref, slo_ref, shi_ref, qn_ref, qr_ref, gt_ref):
    hn = _rms(h_ref[...], g_ref[...]).astype(BF16)
    u = _dot(hn, w_ref[...])
    cos, slo, shi = cos_ref[...], slo_ref[...], shi_ref[...]
    scale = D_QK ** -0.5 * LOG2E
    for hd in range(N_HEADS):
        q = u[:, hd * SLOT:(hd + 1) * SLOT] * scale
        qn_ref[:, hd * SLOT:(hd + 1) * SLOT] = q.astype(BF16)
        qr_ref[:, hd * SLOT:(hd + 1) * SLOT] = _rope_slot(q, cos, slo, shi).astype(BF16)
    gt_ref[...] = jax.nn.sigmoid(u[:, N_HEADS * SLOT:] + gb_ref[...])


def _pad_slots(w, n, width, left):
    d = w.shape[0]
    w = w.reshape(d, n, width)
    return jnp.pad(w, ((0, 0), (0, 0), (left, SLOT - width - left))).reshape(d, n * SLOT)


def _rope_tables(s):
    inv = ROPE_THETA ** (-jnp.arange(ROPE_HALF, dtype=F32) * 2.0 / ROPE_DIMS)
    ang = jnp.arange(s, dtype=F32)[:, None] * inv[None, :]
    cos, sin = jnp.cos(ang), jnp.sin(ang)
    z = lambda n: jnp.zeros((s, n), F32)
    o = lambda n: jnp.ones((s, n), F32)
    rest = SLOT - AUG - ROPE_DIMS
    cos_t = jnp.concatenate([o(AUG), cos, cos, o(rest)], axis=1)
    sin_lo = jnp.concatenate([z(AUG), -sin, z(ROPE_HALF), z(rest)], axis=1)
    sin_hi = jnp.concatenate([z(AUG), z(ROPE_HALF), sin, z(rest)], axis=1)
    return cos_t, sin_lo, sin_hi


def _kv_proj(h, g, kv_w, tables, *, nb, s):
    n, d = h.shape
    g_, dk, dv = N_KV_GROUPS, D_QK, D_V
    tr = 512
    assert n % tr == 0 and s % tr == 0 and tr % CMP_STRIDE == 0
    sizes = [g_ * dk, g_ * dv] * 3
    offs = np.concatenate([[0], np.cumsum(sizes)])
    kc, vc, ksl, vsl, kwn, vwn = [kv_w[:, offs[j]:offs[j + 1]] for j in range(6)]
    assert 2 * dv == SLOT and g_ % 2 == 0
    w = jnp.concatenate([_pad_slots(kc, g_, dk, 0), _pad_slots(vc, g_, dv, 0),
                         _pad_slots(ksl, g_, dk, AUG), _pad_slots(kwn, g_, dk, AUG), vsl, vwn],
                        axis=1).astype(BF16)
    row = lambda wd, dt: (pl.BlockSpec((tr, wd), lambda i: (i, 0)), jax.ShapeDtypeStruct((n, wd), dt))
    nj = tr // CMP_STRIDE
    spt = s // tr
    xcmp = (pl.BlockSpec((None, g_, nj, CMP_STRIDE * SLOT), lambda i: (i // spt, 0, i % spt, 0)),
            jax.ShapeDtypeStruct((nb, g_, s // CMP_STRIDE, CMP_STRIDE * SLOT), F32))
    outs = [xcmp, xcmp, row(g_ * SLOT, BF16), row(g_ * SLOT, BF16), row(g_ * SLOT, BF16), row(g_ * SLOT, BF16)]
    tab_spec = pl.BlockSpec((tr, SLOT), lambda i: (i % spt, 0))
    return pl.pallas_call(
        functools.partial(_kv_proj_kernel, tr=tr, s=s),
        grid=(n // tr,),
        in_specs=[pl.BlockSpec((tr, d), lambda i: (i, 0)), _const_spec((1, d)), _const_spec(w.shape),
                  tab_spec, tab_spec, tab_spec],
        out_specs=[o[0] for o in outs], out_shape=[o[1] for o in outs],
        scratch_shapes=[pltpu.VMEM((2 * g_, tr, SLOT), F32)],
        compiler_params=_cparams(("parallel",)), name="kv_proj",
    )(h, g.reshape(1, d), w, *tables)


def _q_proj(h, g, w_q, gate_bias, tables, *, s):
    n, d = h.shape
    tr = 512
    nq = N_HEADS * D_QK
    ngate = N_HEADS * N_BRANCH
    assert ngate <= LANES
    w = jnp.concatenate([_pad_slots(w_q[:, :nq], N_HEADS, D_QK, AUG),
                         jnp.pad(w_q[:, nq:], ((0, 0), (0, LANES - ngate)))], axis=1).astype(BF16)
    gb = jnp.pad(gate_bias.reshape(1, ngate), ((0, 0), (0, LANES - ngate)))
    tab_spec = pl.BlockSpec((tr, SLOT), lambda i: (i % (s // tr), 0))
    row = lambda wd, dt: (pl.BlockSpec((tr, wd), lambda i: (i, 0)), jax.ShapeDtypeStruct((n, wd), dt))
    outs = [row(N_HEADS * SLOT, BF16), row(N_HEADS * SLOT, BF16), row(LANES, F32)]
    return pl.pallas_call(
        _q_proj_kernel,
        grid=(n // tr,),
        in_specs=[pl.BlockSpec((tr, d), lambda i: (i, 0)), _const_spec((1, d)), _const_spec(w.shape),
                  _const_spec((1, LANES)), tab_spec, tab_spec, tab_spec],
        out_specs=[o[0] for o in outs], out_shape=[o[1] for o in outs],
        compiler_params=_cparams(("parallel",)), name="q_proj",
    )(h, g.reshape(1, d), w, gb, *tables)


def _compress_kernel(x_ref, plo_ref, phi_ref, w1lo_ref, w1hi_ref, w2_ref, o_ref):
    x = x_ref[...]
    lo = _dot((x + plo_ref[...]).astype(BF16), w1lo_ref[...])
    hi = _dot((x + phi_ref[...]).astype(BF16), w1hi_ref[...])
    pre = lo + pltpu.roll(hi, hi.shape[0] - 1, 0)
    o_ref[...] = _dot(_gelu(pre).astype(BF16), w2_ref[...]).astype(BF16)


def _compress(x, pos, w1, w2_slot, *, nchunk):
    n, width = x.shape
    dh = pos.shape[1]
    hid = w1.shape[1]
    assert CMP_BLOCK == 2 * CMP_STRIDE and width == CMP_STRIDE * SLOT
    tr = 4 * nchunk
    assert n % tr == 0

    def slots(v):
        k = v.shape[1]
        v = jnp.pad(v.reshape(2, CMP_STRIDE, dh, k), ((0, 0), (0, 0), (0, SLOT - dh), (0, 0)))
        return v.reshape(2, width, k)

    p = slots(pos.reshape(CMP_BLOCK * dh, 1))[:, :, 0].reshape(2, 1, width)
    w1s = slots(w1).astype(BF16)
    return pl.pallas_call(
        _compress_kernel,
        grid=(n // tr,),
        in_specs=[pl.BlockSpec((tr, width), lambda i: (i, 0)), _const_spec((1, width)), _const_spec((1, width)),
                  _const_spec((width, hid)), _const_spec((width, hid)), _const_spec((hid, SLOT))],
        out_specs=pl.BlockSpec((tr, SLOT), lambda i: (i, 0)),
        out_shape=jax.ShapeDtypeStruct((n, SLOT), BF16),
        compiler_params=_cparams(("parallel",)), name="compress",
    )(x, p[0], p[1], w1s[0], w1s[1], w2_slot.astype(BF16))


def _split3(x):
    hi = x.astype(BF16)
    r1 = x - hi.astype(F32)
    mid = r1.astype(BF16)
    lo = (r1 - mid.astype(F32)).astype(BF16)
    return hi, mid, lo


def _nsa_kernel(qn_ref, qr_ref, gt_ref, kc_ref, vc_ref, ks_ref, vs_ref, kw_ref, vw_ref, ov_ref, ex_ref,
                o_ref, s_ref, mx_ref, acc_ref, kwp_ref, vwp_ref, *, nc, ns, ktop):
    g_ = N_KV_GROUPS
    tq = Q_BLOCK
    rows = HPG * tq
    kch = 2 * tq
    wlen = WINDOW + tq
    qi = pl.program_id(1)
    s0 = pl.multiple_of(qi * tq, tq)
    hw = N_HEADS * D_V
    lane = lax.broadcasted_iota(jnp.int32, (tq, SLOT), 1)

    @pl.when(qi == 0)
    def _():
        pad_lane = lax.broadcasted_iota(jnp.int32, (WINDOW, g_ * SLOT), 1) & (SLOT - 1)
        kwp_ref[0:WINDOW, :] = (pad_lane == 0).astype(BF16)
        vwp_ref[0:WINDOW, :] = jnp.zeros((WINDOW, g_ * SLOT), BF16)
        kwp_ref[WINDOW:, :] = kw_ref[...]
        vwp_ref[WINDOW:, :] = vw_ref[...]

    ghi, gmid, _ = _split3(gt_ref[...])
    gx = _dot(ghi, ex_ref[...]) + _dot(gmid, ex_ref[...])

    r_row = lax.broadcasted_iota(jnp.int32, (rows, 1), 0) & (tq - 1)
    stack = lambda ref, g: [ref[:, hd * SLOT:(hd + 1) * SLOT] for hd in range(g * HPG, (g + 1) * HPG)]

    ncp = kc_ref.shape[1]
    c_idx = lax.broadcasted_iota(jnp.int32, (ncp, 1), 0)
    c_end = jnp.where(c_idx < nc, c_idx * CMP_STRIDE + (CMP_BLOCK - 1), jnp.iinfo(jnp.int32).max)
    t_col = s0 + (lax.broadcasted_iota(jnp.int32, (1, rows), 1) & (tq - 1))
    valid = c_end <= t_col
    o_c, psums = [], []
    for g in range(g_):
        st = _dot_nt(kc_ref[g], jnp.concatenate(stack(qn_ref, g), axis=0))
        st = jnp.where(valid, st, NEG)
        e = jnp.exp2(st - jnp.maximum(jnp.max(st, axis=0, keepdims=True), 0.1 * NEG))
        den = jnp.sum(e, axis=0, keepdims=True)
        pt = e * (1.0 / jnp.where(den > 0, den, 1.0))
        o_c.append(_dot(pt.astype(BF16).T, vc_ref[g]))
        psum = pt[:, 0:tq]
        for i in range(1, HPG):
            psum = psum + pt[:, i * tq:(i + 1) * tq]
        psums.append(psum)

    psum = jnp.concatenate(psums, axis=1)
    imp = sum(_dot(ov_ref[...], part) for part in _split3(psum))[0:ns, :]
    n_i = lax.broadcasted_iota(jnp.int32, imp.shape, 0)
    tt = s0 + (lax.broadcasted_iota(jnp.int32, imp.shape, 1) & (tq - 1))
    cur = tt // SEL_BLOCK
    sel = (n_i == 0) | (n_i == cur) | (n_i == cur - 1)
    score = jnp.where(sel | (n_i * SEL_BLOCK > tt), -jnp.inf, imp)
    for _ in range(ktop - 3):
        mx = jnp.max(score, axis=0, keepdims=True)
        idx = jnp.min(jnp.where(score == mx, n_i, ns), axis=0, keepdims=True)
        hit = n_i == idx
        sel = sel | hit
        score = jnp.where(hit, -jnp.inf, score)
    bias_t = jnp.where(sel, 0.0, SEL_BIAS).astype(BF16)
    bias_t = jnp.concatenate([bias_t, jnp.zeros((SLOT - ns, g_ * tq), BF16)], axis=0)

    qa = []
    for g in range(g_):
        bias = bias_t[:, g * tq:(g + 1) * tq].T
        qa.append(jnp.concatenate([q + bias for q in stack(qr_ref, g)], axis=0))

    nchunk = (qi + 2) // 2
    dif = lax.broadcasted_iota(jnp.int32, (rows, kch), 1) - r_row

    def score_chunk(j, causal):
        k0 = pl.multiple_of(j * kch, kch)
        for g in range(g_):
            sc = _dot_nt(qa[g], ks_ref[pl.ds(k0, kch), g * SLOT:(g + 1) * SLOT])
            if causal:
                sc = jnp.where(dif <= s0 - k0, sc, NEG)
            s_ref[g, j] = sc
            mx_ref[g] = jnp.maximum(mx_ref[g], jnp.maximum(sc[:, :SLOT], sc[:, SLOT:]))

    mx_ref[...] = jnp.full(mx_ref.shape, NEG, F32)
    nfull = nchunk - 1

    def body1(j2, carry):
        score_chunk(2 * j2, False)
        score_chunk(2 * j2 + 1, False)
        return carry

    lax.fori_loop(0, nfull // 2, body1, 0)

    @pl.when(nfull % 2 == 1)
    def _():
        score_chunk(nfull - 1, False)

    score_chunk(nchunk - 1, True)
    for g in range(g_):
        mx_ref[g] = jnp.broadcast_to(jnp.max(mx_ref[g], axis=1, keepdims=True), (rows, SLOT))
    acc_ref[...] = jnp.zeros(acc_ref.shape, F32)

    def pv_chunks(j, nk):
        k0 = pl.multiple_of(j * kch, kch)
        ones_k = jnp.ones((nk * kch, SLOT), BF16)
        for g in range(g_):
            m = mx_ref[g]
            parts = []
            for i in range(nk):
                sc = s_ref[g, j + i]
                parts += [jnp.exp2(sc[:, :SLOT] - m), jnp.exp2(sc[:, SLOT:] - m)]
            p = jnp.concatenate(parts, axis=1).astype(BF16)
            v = jnp.concatenate([vs_ref[pl.ds(k0, nk * kch), g * SLOT:(g + 1) * SLOT], ones_k], axis=1)
            acc_ref[g] += _dot(p, v)

    def body2(j2, carry):
        pv_chunks(2 * j2, 2)
        return carry

    lax.fori_loop(0, nchunk // 2, body2, 0)

    @pl.when(nchunk % 2 == 1)
    def _():
        pv_chunks(nchunk - 1, 1)

    ones_w = jnp.ones((wlen, SLOT), BF16)
    nwt = wlen // tq
    wdif = lax.broadcasted_iota(jnp.int32, (rows, tq), 1) - r_row
    pad_bias = jnp.where(lane == 0, SEL_BIAS, 0.0).astype(BF16)
    o_ws = []
    for g in range(g_):
        qw = jnp.concatenate([q + pad_bias for q in stack(qr_ref, g)], axis=0)
        sw = _dot_nt(qw, kwp_ref[pl.ds(s0, wlen), g * SLOT:(g + 1) * SLOT])
        tiles = [sw[:, i * tq:(i + 1) * tq] for i in range(nwt)]
        tiles[0] = jnp.where(wdif > 0, tiles[0], NEG)
        tiles[-1] = jnp.where(wdif <= 0, tiles[-1], NEG)
        mw = tiles[0]
        for t in tiles[1:]:
            mw = jnp.maximum(mw, t)
        mw = jnp.max(mw, axis=1, keepdims=True)
        pw = jnp.concatenate([jnp.exp2(t - mw) for t in tiles], axis=1).astype(BF16)
        vwin = jnp.concatenate([vwp_ref[pl.ds(s0, wlen), g * SLOT:(g + 1) * SLOT], ones_w], axis=1)
        accw = _dot(pw, vwin)
        o_ws.append(accw[:, :SLOT] / accw[:, SLOT:])

    for g in range(g_):
        acc = acc_ref[g]
        o_s = acc[:, :SLOT] / acc[:, SLOT:]
        o_w = o_ws[g]
        for hp in range(HPG // 2):
            col = (g * HPG + 2 * hp) * D_V
            ev, od = slice(2 * hp * tq, (2 * hp + 1) * tq), slice((2 * hp + 1) * tq, (2 * hp + 2) * tq)
            tile = jnp.zeros((tq, SLOT), F32)
            for br, o_b in enumerate((o_c[g], o_s, o_w)):
                pair = jnp.where(lane < D_V, o_b[ev], o_b[od])
                tile = tile + gx[:, br * hw + col:br * hw + col + SLOT] * pair
            o_ref[:, col:col + SLOT] = tile.astype(BF16)


def _nsa_attn(qn, qr, gates, kc, vc, ks, vs, kw, vw, *, nb, s):
    tq = Q_BLOCK
    nq = s // tq
    nc = (s - CMP_BLOCK) // CMP_STRIDE + 1
    ncp = s // CMP_STRIDE
    ns = s // SEL_BLOCK
    ktop = min(SEL_TOPK, ns)
    g_ = N_KV_GROUPS
    assert ncp % 8 == 0 and ns <= AUG and ktop >= 3 and s % (2 * tq) == 0 and 2 * D_V == SLOT
    hw = N_HEADS * D_V
    cs = np.arange(ncp) * CMP_STRIDE
    ss = np.arange(SLOT) * SEL_BLOCK
    ov = ((cs[None, :] < ss[:, None] + SEL_BLOCK) & (cs[None, :] + CMP_BLOCK > ss[:, None])
          & (np.arange(ncp)[None, :] < nc) & (np.arange(SLOT)[:, None] < ns))
    ex = np.zeros((LANES, N_BRANCH * hw), np.float32)
    for hd in range(N_HEADS):
        for br in range(N_BRANCH):
            ex[hd * N_BRANCH + br, br * hw + hd * D_V:br * hw + (hd + 1) * D_V] = 1.0
    qspec = lambda wd: pl.BlockSpec((tq, wd), lambda b, q: (b * nq + q, 0))
    seq = lambda wd: pl.BlockSpec((s, wd), lambda b, q: (b, 0))
    cmp_spec = pl.BlockSpec((None, g_, ncp, SLOT), lambda b, q: (b, 0, 0, 0))
    nkc = s // (2 * tq)
    return pl.pallas_call(
        functools.partial(_nsa_kernel, nc=nc, ns=ns, ktop=ktop),
        grid=(nb, nq),
        in_specs=[qspec(N_HEADS * SLOT), qspec(N_HEADS * SLOT), qspec(LANES), cmp_spec, cmp_spec,
                  seq(g_ * SLOT), seq(g_ * SLOT), seq(g_ * SLOT), seq(g_ * SLOT),
                  _const_spec(ov.shape), _const_spec(ex.shape)],
        out_specs=qspec(hw),
        out_shape=jax.ShapeDtypeStruct((nb * s, hw), BF16),
        scratch_shapes=[pltpu.VMEM((g_, nkc, HPG * tq, 2 * tq), F32),
                        pltpu.VMEM((g_, HPG * tq, SLOT), F32),
                        pltpu.VMEM((g_, HPG * tq, 2 * SLOT), F32),
                        pltpu.VMEM((s + WINDOW, g_ * SLOT), BF16),
                        pltpu.VMEM((s + WINDOW, g_ * SLOT), BF16)],
        compiler_params=_cparams(("parallel", "arbitrary")), name="nsa_attn",
    )(qn, qr, gates, kc.reshape(nb, g_, ncp, SLOT), vc.reshape(nb, g_, ncp, SLOT), ks, vs, kw, vw,
      jnp.asarray(ov, BF16), jnp.asarray(ex, BF16))


def kernel(x, norm_g, ffn_w_gu, ffn_w_down, a_w_in, a_conv_w, a_conv_b, a_w_ra, a_b_ra, a_w_ix, a_b_ix,
           a_lambda, a_w_out, kv_norm_g, kv_w, cmp_pos_k, cmp_w1_k, cmp_w2_k, cmp_pos_v, cmp_w1_v,
           cmp_w2_v, b_w_q, b_gate_bias, b_w_o, final_g):
    nb, s, d = x.shape
    n_a, n_b = a_w_in.shape[0], b_w_q.shape[0]
    depth = n_a + n_b
    assert n_a >= 1 and norm_g.shape[0] == depth

    h = x
    for layer in range(n_a):
        hm = _rglru_mixer(h, norm_g[layer, 0], a_w_in[layer], a_conv_w[layer], a_conv_b[layer],
                          a_w_ra[layer], a_b_ra[layer], a_w_ix[layer], a_b_ix[layer], a_lambda[layer],
                          a_w_out[layer])
        h = _ffn(hm.reshape(nb * s, d), norm_g[layer, 1], ffn_w_gu[layer], ffn_w_down[layer],
                 final_g=final_g if layer == depth - 1 else None).reshape(nb, s, d)
    if n_b == 0:
        return h
    h = h.reshape(nb * s, d)

    tables = _rope_tables(s)
    xk, xv, ks, vs, kw, vw = _kv_proj(h, kv_norm_g, kv_w, tables, nb=nb, s=s)
    nchunk = s // CMP_STRIDE
    w2k = jnp.pad(cmp_w2_k, ((0, 0), (AUG, 0)))
    w2v = jnp.tile(cmp_w2_v, (1, SLOT // D_V))
    kc = _compress(xk.reshape(-1, xk.shape[-1]), cmp_pos_k, cmp_w1_k, w2k, nchunk=nchunk)
    vc = _compress(xv.reshape(-1, xv.shape[-1]), cmp_pos_v, cmp_w1_v, w2v, nchunk=nchunk)

    for j in range(n_b):
        layer = n_a + j
        qn, qr, gates = _q_proj(h, norm_g[layer, 0], b_w_q[j], b_gate_bias[j], tables, s=s)
        attn = _nsa_attn(qn, qr, gates, kc, vc, ks, vs, kw, vw, nb=nb, s=s)
        h = _ffn(h, norm_g[layer, 1], ffn_w_gu[layer], ffn_w_down[layer], attn=attn, w_o=b_w_o[j],
                 final_g=final_g if layer == depth - 1 else None)
    return h.reshape(nb, s, d)
```

```python
import functools
import math

import numpy as np
import jax
import jax.numpy as jnp
from jax import lax
from jax.experimental import pallas as pl
from jax.experimental.pallas import tpu as pltpu

CONV_WIDTH = 4
RGLRU_C = 8.0
N_HEADS = 16
N_KV_GROUPS = 4
HPG = N_HEADS // N_KV_GROUPS
D_QK = 96
D_V = 64
ROPE_DIMS = D_QK // 4
ROPE_HALF = ROPE_DIMS // 2
ROPE_THETA = 500000.0
CMP_BLOCK = 32
CMP_STRIDE = 16
SEL_BLOCK = 64
SEL_TOPK = 8
WINDOW = 512
Q_BLOCK = 128
N_BRANCH = 3
EPS = 1e-6

LANES = 128
MXU_TILE = 256
SLOT = 128
AUG = SLOT - D_QK
VMEM_LIMIT = 56 * 1024 * 1024
NEG = -1e30
SEL_BIAS = -1e9
LOG2E = math.log2(math.e)
UNROLL = 4

F32 = jnp.float32
BF16 = jnp.bfloat16


def _cparams(sem):
    return pltpu.CompilerParams(dimension_semantics=sem, vmem_limit_bytes=VMEM_LIMIT)


def _const_spec(shape):
    nd = len(shape)
    return pl.BlockSpec(shape, lambda *_: (0,) * nd, pipeline_mode=pl.Buffered(1))


def _rms(x, g):
    return x * lax.rsqrt(jnp.mean(x * x, axis=-1, keepdims=True) + EPS) * g


def _gelu(x):
    k1 = -2.0 * math.sqrt(2.0 / math.pi) * LOG2E
    return x / (1.0 + jnp.exp2(x * (k1 + (k1 * 0.044715) * (x * x))))


def _dot(a, b):
    return jnp.dot(a, b, preferred_element_type=F32)


def _dot_nt(a, b):
    return lax.dot_general(a, b, (((1,), (1,)), ((), ())), preferred_element_type=F32)


def _rglru_kernel(x_hbm, g_ref, win_ref, cw_ref, cb_ref, wg_ref, bra_ref, bix_ref, lam_ref, wout_ref,
                  o_hbm, xbe_ref, h_ref, ry_ref, xt_ref, ot_ref, sem_in, sem_out, *, ts, nb, cw, nch, nsteps):
    rows = ts * nb
    halo = (CONV_WIDTH - 1) * nb
    d = xt_ref.shape[-1]
    i = pl.program_id(0)
    slot = i % 2

    def in_copies(step, sl):
        return [pltpu.make_async_copy(x_hbm.at[:, step * ts + t, :], xt_ref.at[sl, t], sem_in.at[sl])
                for t in range(ts)]

    def out_copies(step, sl):
        return [pltpu.make_async_copy(ot_ref.at[sl, t], o_hbm.at[:, step * ts + t, :], sem_out.at[sl])
                for t in range(ts)]

    @pl.when(i == 0)
    def _():
        xbe_ref[:, 0:halo, :] = jnp.zeros((nch, halo, cw), F32)
        h_ref[...] = jnp.zeros_like(h_ref)
        for cp in in_copies(0, 0):
            cp.start()

    @pl.when(i + 1 < nsteps)
    def _():
        for cp in in_copies(i + 1, 1 - slot):
            cp.start()

    for cp in in_copies(i, slot):
        cp.wait()

    hn = _rms(xt_ref[slot].reshape(rows, d), g_ref[...]).astype(BF16)
    for c in range(nch):
        u = _dot(hn, win_ref[c])
        xbe_ref[c, halo:halo + rows, :] = u[:, :cw]
        y = _gelu(u[:, cw:])
        xc = cb_ref[c]
        for j in range(CONV_WIDTH):
            xc = xc + cw_ref[c, j:j + 1, :] * xbe_ref[c, j * nb:j * nb + rows, :]
        xbe_ref[c, 0:halo, :] = xbe_ref[c, rows:rows + halo, :]
        gt = _dot(xc.astype(BF16), wg_ref[c])
        r = jax.nn.sigmoid(gt[:, :cw] + bra_ref[c])
        ig = jax.nn.sigmoid(gt[:, cw:] + bix_ref[c])
        nla = r * (RGLRU_C * jax.nn.softplus(-lam_ref[c]))
        a = jnp.exp2(nla * (-LOG2E))
        om = jnp.tanh(nla) * (1.0 + a * a)
        bx = jnp.where(om > 0, om * lax.rsqrt(om), 0.0) * (ig * xc)
        h = h_ref[c]
        for t in range(ts):
            sl = slice(t * nb, (t + 1) * nb)
            h = a[sl] * h + bx[sl]
            ry_ref[sl, c * cw:(c + 1) * cw] = (h * y[sl]).astype(BF16)
        h_ref[c] = h
    out = _dot(ry_ref[...], wout_ref[...]) + xt_ref[slot].reshape(rows, d)

    @pl.when(i >= 2)
    def _():
        for cp in out_copies(i - 2, slot):
            cp.wait()

    ot_ref[slot] = out.reshape(ts, nb, d)
    for cp in out_copies(i, slot):
        cp.start()

    @pl.when(i == nsteps - 1)
    def _():
        for cp in out_copies(i, slot):
            cp.wait()
        if nsteps > 1:
            for cp in out_copies(i - 1, 1 - slot):
                cp.wait()


def _rglru_mixer(x, g, w_in, conv_w, conv_b, w_ra, b_ra, w_ix, b_ix, lam, w_out):
    nb, s, _ = x.shape
    d = w_in.shape[0]
    r = w_in.shape[1] // 2
    nblk, rb = w_ra.shape[0], w_ra.shape[1]
    per = LANES // math.gcd(rb, LANES)
    cw = per * rb
    nch = nblk // per
    assert nch * per == nblk and nch * cw == r
    ts = 16
    assert s % ts == 0 and nb % 16 == 0
    rows = ts * nb

    def chunks(v):
        return v.reshape(v.shape[0], nch, cw).transpose(1, 0, 2)

    def blockdiag(w):
        w = w.reshape(nch, per, rb, rb)
        return jnp.einsum('cpij,pq->cpiqj', w, jnp.eye(per, dtype=w.dtype)).reshape(nch, cw, cw)

    win = jnp.concatenate([chunks(w_in[:, :r]), chunks(w_in[:, r:])], axis=-1).astype(BF16)
    wg = jnp.concatenate([blockdiag(w_ra), blockdiag(w_ix)], axis=-1).astype(BF16)
    args = (x, g.reshape(1, d), win, chunks(conv_w), chunks(conv_b.reshape(1, r)), wg,
            chunks(b_ra.reshape(1, r)), chunks(b_ix.reshape(1, r)), chunks(lam.reshape(1, r)),
            w_out.astype(BF16))
    hbm = pl.BlockSpec(memory_space=pl.ANY)
    in_specs = [hbm] + [_const_spec(a.shape) for a in args[1:]]
    kern = functools.partial(_rglru_kernel, ts=ts, nb=nb, cw=cw, nch=nch, nsteps=s // ts)
    return pl.pallas_call(
        kern,
        grid=(s // ts,),
        in_specs=in_specs,
        out_specs=hbm,
        out_shape=jax.ShapeDtypeStruct((nb, s, d), F32),
        scratch_shapes=[
            pltpu.VMEM((nch, rows + (CONV_WIDTH - 1) * nb, cw), F32),
            pltpu.VMEM((nch, nb, cw), F32),
            pltpu.VMEM((rows, r), BF16),
            pltpu.VMEM((2, ts, nb, d), F32),
            pltpu.VMEM((2, ts, nb, d), F32),
            pltpu.SemaphoreType.DMA((2,)),
            pltpu.SemaphoreType.DMA((2,)),
        ],
        compiler_params=_cparams(("arbitrary",)),
        name="rglru_mixer",
    )(*args)


def _ffn_kernel(*refs, f, bounds, pre_attn, final_norm):
    it = iter(refs)
    h_ref = next(it)
    if pre_attn:
        attn_ref, wo_ref = next(it), next(it)
    g_ref, wgu_ref, wdn_ref = next(it), next(it), next(it)
    if final_norm:
        gf_ref = next(it)
    o_ref = next(it)

    h = h_ref[...]
    if pre_attn:
        h = h + _dot(attn_ref[...], wo_ref[...])
    hn = _rms(h, g_ref[...]).astype(BF16)
    acc = h
    for lo, hi in zip(bounds[:-1], bounds[1:]):
        gate = _dot(hn, wgu_ref[:, lo:hi])
        val = _dot(hn, wgu_ref[:, f + lo:f + hi])
        act = (jax.nn.silu(gate) * val).astype(BF16)
        acc = acc + _dot(act, wdn_ref[lo:hi, :])
    if final_norm:
        acc = _rms(acc, gf_ref[...])
    o_ref[...] = acc


def _ffn(h, g, w_gu, w_down, *, attn=None, w_o=None, final_g=None):
    n, d = h.shape
    f = w_down.shape[0]
    tr = 512
    mid = -(-f // (2 * MXU_TILE)) * MXU_TILE
    bounds = (0, mid, f) if mid < f else (0, f)
    assert n % tr == 0 and f % LANES == 0
    pre_attn = attn is not None
    final_norm = final_g is not None
    row_spec = lambda w: pl.BlockSpec((tr, w), lambda i: (i, 0))
    args, specs = [h], [row_spec(d)]
    if pre_attn:
        args += [attn, w_o.astype(BF16)]
        specs += [row_spec(attn.shape[1]), _const_spec(w_o.shape)]
    args += [g.reshape(1, d), w_gu.astype(BF16), w_down.astype(BF16)]
    specs += [_const_spec((1, d)), _const_spec(w_gu.shape), _const_spec(w_down.shape)]
    if final_norm:
        args.append(final_g.reshape(1, d))
        specs.append(_const_spec((1, d)))
    kern = functools.partial(_ffn_kernel, f=f, bounds=bounds, pre_attn=pre_attn, final_norm=final_norm)
    return pl.pallas_call(
        kern, grid=(n // tr,), in_specs=specs, out_specs=row_spec(d),
        out_shape=jax.ShapeDtypeStruct((n, d), F32),
        compiler_params=_cparams(("parallel",)), name="ffn",
    )(*args)


def _rope_slot(x, cos, sin_lo, sin_hi):
    return (x * cos + pltpu.roll(x, SLOT - ROPE_HALF, 1) * sin_lo
            + pltpu.roll(x, ROPE_HALF, 1) * sin_hi)


def _kv_proj_kernel(h_ref, g_ref, w_ref, cos_ref, slo_ref, shi_ref,
                    xk_ref, xv_ref, ks_ref, vs_ref, kw_ref, vw_ref, sc_ref, *, tr, s):
    g_ = N_KV_GROUPS
    hn = _rms(h_ref[...], g_ref[...]).astype(BF16)
    u = _dot(hn, w_ref[...])
    slot = lambda i: u[:, i * SLOT:(i + 1) * SLOT]
    cos, slo, shi = cos_ref[...], slo_ref[...], shi_ref[...]
    pos0 = (pl.program_id(0) % (s // tr)) * tr
    pos = pos0 + lax.broadcasted_iota(jnp.int32, (tr, SLOT), 0)
    lane = lax.broadcasted_iota(jnp.int32, (tr, SLOT), 1)
    onehot = (lane == pos // SEL_BLOCK).astype(F32)

    nj = tr // CMP_STRIDE
    for i in range(2 * g_):
        sc_ref[i] = slot(i)
    for x_ref, base in ((xk_ref, 0), (xv_ref, g_)):
        for gi in range(g_):
            for l in range(CMP_STRIDE):
                x_ref[gi, :, l * SLOT:(l + 1) * SLOT] = sc_ref[base + gi, pl.ds(l, nj, stride=CMP_STRIDE), :]

    vbase = 4 * g_
    for k_ref, v_ref, base, aug in ((ks_ref, vs_ref, 2 * g_, True), (kw_ref, vw_ref, 3 * g_, False)):
        for gi in range(g_):
            k = _rope_slot(slot(base + gi), cos, slo, shi)
            if aug:
                k = k + onehot
            k_ref[:, gi * SLOT:(gi + 1) * SLOT] = k.astype(BF16)
        for pair in range(g_ // 2):
            x = slot(vbase + pair)
            y = pltpu.roll(x, D_V, 1)
            v_ref[:, 2 * pair * SLOT:(2 * pair + 1) * SLOT] = jnp.where(lane < D_V, x, y).astype(BF16)
            v_ref[:, (2 * pair + 1) * SLOT:(2 * pair + 2) * SLOT] = jnp.where(lane < D_V, y, x).astype(BF16)
        vbase += g_ // 2


def _q_proj_kernel(h_ref, g_ref, w_ref, gb_ref, cos_ref, slo_ref, shi_ref, qn_ref, qr_ref, gt_ref):
    hn = _rms(h_ref[...], g_ref[...]).astype(BF16)
    u = _dot(hn, w_ref[...])
    cos, slo, shi = cos_ref[...], slo_ref[...], shi_ref[...]
    scale = D_QK ** -0.5 * LOG2E
    for hd in range(N_HEADS):
        q = u[:, hd * SLOT:(hd + 1) * SLOT] * scale
        qn_ref[:, hd * SLOT:(hd + 1) * SLOT] = q.astype(BF16)
        qr_ref[:, hd * SLOT:(hd + 1) * SLOT] = _rope_slot(q, cos, slo, shi).astype(BF16)
    gt_ref[...] = jax.nn.sigmoid(u[:, N_HEADS * SLOT:] + gb_ref[...])


def _pad_slots(w, n, width, left):
    d = w.shape[0]
    w = w.reshape(d, n, width)
    return jnp.pad(w, ((0, 0), (0, 0), (left, SLOT - width - left))).reshape(d, n * SLOT)


def _rope_tables(s):
    inv = ROPE_THETA ** (-jnp.arange(ROPE_HALF, dtype=F32) * 2.0 / ROPE_DIMS)
    ang = jnp.arange(s, dtype=F32)[:, None] * inv[None, :]
    cos, sin = jnp.cos(ang), jnp.sin(ang)
    z = lambda n: jnp.zeros((s, n), F32)
    o = lambda n: jnp.ones((s, n), F32)
    rest = SLOT - AUG - ROPE_DIMS
    cos_t = jnp.concatenate([o(AUG), cos, cos, o(rest)], axis=1)
    sin_lo = jnp.concatenate([z(AUG), -sin, z(ROPE_HALF), z(rest)], axis=1)
    sin_hi = jnp.concatenate([z(AUG), z(ROPE_HALF), sin, z(rest)], axis=1)
    return cos_t, sin_lo, sin_hi


def _kv_proj(h, g, kv_w, tables, *, nb, s):
    n, d = h.shape
    g_, dk, dv = N_KV_GROUPS, D_QK, D_V
    tr = 1024
    assert n % tr == 0 and s % tr == 0 and tr % CMP_STRIDE == 0
    sizes = [g_ * dk, g_ * dv] * 3
    offs = np.concatenate([[0], np.cumsum(sizes)])
    kc, vc, ksl, vsl, kwn, vwn = [kv_w[:, offs[j]:offs[j + 1]] for j in range(6)]
    assert 2 * dv == SLOT and g_ % 2 == 0
    w = jnp.concatenate([_pad_slots(kc, g_, dk, 0), _pad_slots(vc, g_, dv, 0),
                         _pad_slots(ksl, g_, dk, AUG), _pad_slots(kwn, g_, dk, AUG), vsl, vwn],
                        axis=1).astype(BF16)
    row = lambda wd, dt: (pl.BlockSpec((tr, wd), lambda i: (i, 0)), jax.ShapeDtypeStruct((n, wd), dt))
    nj = tr // CMP_STRIDE
    spt = s // tr
    xcmp = (pl.BlockSpec((None, g_, nj, CMP_STRIDE * SLOT), lambda i: (i // spt, 0, i % spt, 0)),
            jax.ShapeDtypeStruct((nb, g_, s // CMP_STRIDE, CMP_STRIDE * SLOT), F32))
    outs = [xcmp, xcmp, row(g_ * SLOT, BF16), row(g_ * SLOT, BF16), row(g_ * SLOT, BF16), row(g_ * SLOT, BF16)]
    tab_spec = pl.BlockSpec((tr, SLOT), lambda i: (i % spt, 0))
    return pl.pallas_call(
        functools.partial(_kv_proj_kernel, tr=tr, s=s),
        grid=(n // tr,),
        in_specs=[pl.BlockSpec((tr, d), lambda i: (i, 0)), _const_spec((1, d)), _const_spec(w.shape),
                  tab_spec, tab_spec, tab_spec],
        out_specs=[o[0] for o in outs], out_shape=[o[1] for o in outs],
        scratch_shapes=[pltpu.VMEM((2 * g_, tr, SLOT), F32)],
        compiler_params=_cparams(("parallel",)), name="kv_proj",
    )(h, g.reshape(1, d), w, *tables)


def _q_proj(h, g, w_q, gate_bias, tables, *, s):
    n, d = h.shape
    tr = 1024
    nq = N_HEADS * D_QK
    ngate = N_HEADS * N_BRANCH
    assert ngate <= LANES
    w = jnp.concatenate([_pad_slots(w_q[:, :nq], N_HEADS, D_QK, AUG),
                         jnp.pad(w_q[:, nq:], ((0, 0), (0, LANES - ngate)))], axis=1).astype(BF16)
    gb = jnp.pad(gate_bias.reshape(1, ngate), ((0, 0), (0, LANES - ngate)))
    tab_spec = pl.BlockSpec((tr, SLOT), lambda i: (i % (s // tr), 0))
    row = lambda wd, dt: (pl.BlockSpec((tr, wd), lambda i: (i, 0)), jax.ShapeDtypeStruct((n, wd), dt))
    outs = [row(N_HEADS * SLOT, BF16), row(N_HEADS * SLOT, BF16), row(LANES, F32)]
    return pl.pallas_call(
        _q_proj_kernel,
        grid=(n // tr,),
        in_specs=[pl.BlockSpec((tr, d), lambda i: (i, 0)), _const_spec((1, d)), _const_spec(w.shape),
                  _const_spec((1, LANES)), tab_spec, tab_spec, tab_spec],
        out_specs=[o[0] for o in outs], out_shape=[o[1] for o in outs],
        compiler_params=_cparams(("parallel",)), name="q_proj",
    )(h, g.reshape(1, d), w, gb, *tables)


def _compress_kernel(x_ref, plo_ref, phi_ref, w1lo_ref, w1hi_ref, w2_ref, o_ref):
    x = x_ref[...]
    lo = _dot((x + plo_ref[...]).astype(BF16), w1lo_ref[...])
    hi = _dot((x + phi_ref[...]).astype(BF16), w1hi_ref[...])
    pre = lo + pltpu.roll(hi, hi.shape[0] - 1, 0)
    o_ref[...] = _dot(_gelu(pre).astype(BF16), w2_ref[...]).astype(BF16)


def _compress(x, pos, w1, w2_slot, *, nchunk):
    n, width = x.shape
    dh = pos.shape[1]
    hid = w1.shape[1]
    assert CMP_BLOCK == 2 * CMP_STRIDE and width == CMP_STRIDE * SLOT
    tr = 4 * nchunk
    assert n % tr == 0

    def slots(v):
        k = v.shape[1]
        v = jnp.pad(v.reshape(2, CMP_STRIDE, dh, k), ((0, 0), (0, 0), (0, SLOT - dh), (0, 0)))
        return v.reshape(2, width, k)

    p = slots(pos.reshape(CMP_BLOCK * dh, 1))[:, :, 0].reshape(2, 1, width)
    w1s = slots(w1).astype(BF16)
    return pl.pallas_call(
        _compress_kernel,
        grid=(n // tr,),
        in_specs=[pl.BlockSpec((tr, width), lambda i: (i, 0)), _const_spec((1, width)), _const_spec((1, width)),
                  _const_spec((width, hid)), _const_spec((width, hid)), _const_spec((hid, SLOT))],
        out_specs=pl.BlockSpec((tr, SLOT), lambda i: (i, 0)),
        out_shape=jax.ShapeDtypeStruct((n, SLOT), BF16),
        compiler_params=_cparams(("parallel",)), name="compress",
    )(x, p[0], p[1], w1s[0], w1s[1], w2_slot.astype(BF16))


def _split3(x):
    hi = x.astype(BF16)
    r1 = x - hi.astype(F32)
    mid = r1.astype(BF16)
    lo = (r1 - mid.astype(F32)).astype(BF16)
    return hi, mid, lo


def _nsa_kernel(qn_ref, qr_ref, gt_ref, kc_ref, vc_ref, ks_ref, vs_ref, kw_ref, vw_ref, ov_ref, ex_ref,
                o_ref, s_ref, mx_ref, acc_ref, kwp_ref, vwp_ref, *, nc, ns, ktop):
    g_ = N_KV_GROUPS
    tq = Q_BLOCK
    rows = HPG * tq
    kch = 2 * tq
    wlen = WINDOW + tq
    qi = pl.program_id(1)
    s0 = pl.multiple_of(qi * tq, tq)
    hw = N_HEADS * D_V
    lane = lax.broadcasted_iota(jnp.int32, (tq, SLOT), 1)

    @pl.when(qi == 0)
    def _():
        pad_lane = lax.broadcasted_iota(jnp.int32, (WINDOW, g_ * SLOT), 1) & (SLOT - 1)
        kwp_ref[0:WINDOW, :] = (pad_lane == 0).astype(BF16)
        vwp_ref[0:WINDOW, :] = jnp.zeros((WINDOW, g_ * SLOT), BF16)
        kwp_ref[WINDOW:, :] = kw_ref[...]
        vwp_ref[WINDOW:, :] = vw_ref[...]

    ghi, gmid, _ = _split3(gt_ref[...])
    gx = _dot(ghi, ex_ref[...]) + _dot(gmid, ex_ref[...])

    r_row = lax.broadcasted_iota(jnp.int32, (rows, 1), 0) & (tq - 1)
    stack = lambda ref, g: [ref[:, hd * SLOT:(hd + 1) * SLOT] for hd in range(g * HPG, (g + 1) * HPG)]

    ncp = kc_ref.shape[1]
    c_idx = lax.broadcasted_iota(jnp.int32, (ncp, 1), 0)
    c_end = jnp.where(c_idx < nc, c_idx * CMP_STRIDE + (CMP_BLOCK - 1), jnp.iinfo(jnp.int32).max)
    t_col = s0 + (lax.broadcasted_iota(jnp.int32, (1, rows), 1) & (tq - 1))
    valid = c_end <= t_col
    o_c, psums = [], []
    for g in range(g_):
        st = _dot_nt(kc_ref[g], jnp.concatenate(stack(qn_ref, g), axis=0))
        st = jnp.where(valid, st, NEG)
        e = jnp.exp2(st - jnp.maximum(jnp.max(st, axis=0, keepdims=True), 0.1 * NEG))
        den = jnp.sum(e, axis=0, keepdims=True)
        pt = e * (1.0 / jnp.where(den > 0, den, 1.0))
        o_c.append(_dot(pt.astype(BF16).T, vc_ref[g]))
        psum = pt[:, 0:tq]
        for i in range(1, HPG):
            psum = psum + pt[:, i * tq:(i + 1) * tq]
        psums.append(psum)

    psum = jnp.concatenate(psums, axis=1)
    imp = sum(_dot(ov_ref[...], part) for part in _split3(psum))[0:ns, :]
    n_i = lax.broadcasted_iota(jnp.int32, imp.shape, 0)
    tt = s0 + (lax.broadcasted_iota(jnp.int32, imp.shape, 1) & (tq - 1))
    cur = tt // SEL_BLOCK
    sel = (n_i == 0) | (n_i == cur) | (n_i == cur - 1)
    score = jnp.where(sel | (n_i * SEL_BLOCK > tt), -jnp.inf, imp)
    for _ in range(ktop - 3):
        mx = jnp.max(score, axis=0, keepdims=True)
        idx = jnp.min(jnp.where(score == mx, n_i, ns), axis=0, keepdims=True)
        hit = n_i == idx
        sel = sel | hit
        score = jnp.where(hit, -jnp.inf, score)
    bias_t = jnp.where(sel, 0.0, SEL_BIAS).astype(BF16)
    bias_t = jnp.concatenate([bias_t, jnp.zeros((SLOT - ns, g_ * tq), BF16)], axis=0)

    qa = []
    for g in range(g_):
        bias = bias_t[:, g * tq:(g + 1) * tq].T
        qa.append(jnp.concatenate([q + bias for q in stack(qr_ref, g)], axis=0))

    nchunk = (qi + 2) // 2
    dif = lax.broadcasted_iota(jnp.int32, (rows, kch), 1) - r_row

    def run_chunks(n, fn):
        def body(q, carry):
            fn(UNROLL * q, UNROLL)
            return carry

        lax.fori_loop(0, n // UNROLL, body, 0)

        @pl.when(n % UNROLL >= 2)
        def _():
            fn((n // UNROLL) * UNROLL, 2)

        @pl.when(n % 2 == 1)
        def _():
            fn(n - 1, 1)

    def score_chunks(j, nk):
        for g in range(g_):
            mg = mx_ref[g]
            for i in range(nk):
                k0 = pl.multiple_of((j + i) * kch, kch)
                sc = _dot_nt(qa[g], ks_ref[pl.ds(k0, kch), g * SLOT:(g + 1) * SLOT])
                sc = jnp.where(dif <= s0 - k0, sc, NEG)
                s_ref[g, j + i] = sc
                mg = jnp.maximum(mg, jnp.maximum(sc[:, :SLOT], sc[:, SLOT:]))
            mx_ref[g] = mg

    mx_ref[...] = jnp.full(mx_ref.shape, NEG, F32)
    run_chunks(nchunk, score_chunks)
    for g in range(g_):
        mx_ref[g] = jnp.broadcast_to(jnp.max(mx_ref[g], axis=1, keepdims=True), (rows, SLOT))
    acc_ref[...] = jnp.zeros(acc_ref.shape, F32)

    def pv_chunks(j, nk):
        k0 = pl.multiple_of(j * kch, kch)
        ones_k = jnp.ones((nk * kch, SLOT), BF16)
        for g in range(g_):
            m = mx_ref[g]
            parts = []
            for i in range(nk):
                sc = s_ref[g, j + i]
                parts += [jnp.exp2(sc[:, :SLOT] - m), jnp.exp2(sc[:, SLOT:] - m)]
            p = jnp.concatenate(parts, axis=1).astype(BF16)
            v = jnp.concatenate([vs_ref[pl.ds(k0, nk * kch), g * SLOT:(g + 1) * SLOT], ones_k], axis=1)
            acc_ref[g] += _dot(p, v)

    run_chunks(nchunk, pv_chunks)

    ones_w = jnp.ones((wlen, SLOT), BF16)
    nwt = wlen // tq
    wdif = lax.broadcasted_iota(jnp.int32, (rows, tq), 1) - r_row
    pad_bias = jnp.where(lane == 0, SEL_BIAS, 0.0).astype(BF16)
    o_ws = []
    for g in range(g_):
        qw = jnp.concatenate([q + pad_bias for q in stack(qr_ref, g)], axis=0)
        sw = _dot_nt(qw, kwp_ref[pl.ds(s0, wlen), g * SLOT:(g + 1) * SLOT])
        tiles = [sw[:, i * tq:(i + 1) * tq] for i in range(nwt)]
        tiles[0] = jnp.where(wdif > 0, tiles[0], NEG)
        tiles[-1] = jnp.where(wdif <= 0, tiles[-1], NEG)
        mw = tiles[0]
        for t in tiles[1:]:
            mw = jnp.maximum(mw, t)
        mw = jnp.max(mw, axis=1, keepdims=True)
        pw = jnp.concatenate([jnp.exp2(t - mw) for t in tiles], axis=1).astype(BF16)
        vwin = jnp.concatenate([vwp_ref[pl.ds(s0, wlen), g * SLOT:(g + 1) * SLOT], ones_w], axis=1)
        accw = _dot(pw, vwin)
        o_ws.append(accw[:, :SLOT] / accw[:, SLOT:])

    for g in range(g_):
        acc = acc_ref[g]
        o_s = acc[:, :SLOT] / acc[:, SLOT:]
        o_w = o_ws[g]
        for hp in range(HPG // 2):
            col = (g * HPG + 2 * hp) * D_V
            ev, od = slice(2 * hp * tq, (2 * hp + 1) * tq), slice((2 * hp + 1) * tq, (2 * hp + 2) * tq)
            tile = jnp.zeros((tq, SLOT), F32)
            for br, o_b in enumerate((o_c[g], o_s, o_w)):
                pair = jnp.where(lane < D_V, o_b[ev], o_b[od])
                tile = tile + gx[:, br * hw + col:br * hw + col + SLOT] * pair
            o_ref[:, col:col + SLOT] = tile.astype(BF16)


def _nsa_attn(qn, qr, gates, kc, vc, ks, vs, kw, vw, *, nb, s):
    tq = Q_BLOCK
    nq = s // tq
    nc = (s - CMP_BLOCK) // CMP_STRIDE + 1
    ncp = s // CMP_STRIDE
    ns = s // SEL_BLOCK
    ktop = min(SEL_TOPK, ns)
    g_ = N_KV_GROUPS
    assert ncp % 8 == 0 and ns <= AUG and ktop >= 3 and s % (2 * tq) == 0 and 2 * D_V == SLOT
    hw = N_HEADS * D_V
    cs = np.arange(ncp) * CMP_STRIDE
    ss = np.arange(SLOT) * SEL_BLOCK
    ov = ((cs[None, :] < ss[:, None] + SEL_BLOCK) & (cs[None, :] + CMP_BLOCK > ss[:, None])
          & (np.arange(ncp)[None, :] < nc) & (np.arange(SLOT)[:, None] < ns))
    ex = np.zeros((LANES, N_BRANCH * hw), np.float32)
    for hd in range(N_HEADS):
        for br in range(N_BRANCH):
            ex[hd * N_BRANCH + br, br * hw + hd * D_V:br * hw + (hd + 1) * D_V] = 1.0
    qspec = lambda wd: pl.BlockSpec((tq, wd), lambda b, q: (b * nq + q, 0))
    seq = lambda wd: pl.BlockSpec((s, wd), lambda b, q: (b, 0))
    cmp_spec = pl.BlockSpec((None, g_, ncp, SLOT), lambda b, q: (b, 0, 0, 0))
    nkc = s // (2 * tq)
    return pl.pallas_call(
        functools.partial(_nsa_kernel, nc=nc, ns=ns, ktop=ktop),
        grid=(nb, nq),
        in_specs=[qspec(N_HEADS * SLOT), qspec(N_HEADS * SLOT), qspec(LANES), cmp_spec, cmp_spec,
                  seq(g_ * SLOT), seq(g_ * SLOT), seq(g_ * SLOT), seq(g_ * SLOT),
                  _const_spec(ov.shape), _const_spec(ex.shape)],
        out_specs=qspec(hw),
        out_shape=jax.ShapeDtypeStruct((nb * s, hw), BF16),
        scratch_shapes=[pltpu.VMEM((g_, nkc, HPG * tq, 2 * tq), F32),
                        pltpu.VMEM((g_, HPG * tq, SLOT), F32),
                        pltpu.VMEM((g_, HPG * tq, 2 * SLOT), F32),
                        pltpu.VMEM((s + WINDOW, g_ * SLOT), BF16),
                        pltpu.VMEM((s + WINDOW, g_ * SLOT), BF16)],
        compiler_params=_cparams(("parallel", "arbitrary")), name="nsa_attn",
    )(qn, qr, gates, kc.reshape(nb, g_, ncp, SLOT), vc.reshape(nb, g_, ncp, SLOT), ks, vs, kw, vw,
      jnp.asarray(ov, BF16), jnp.asarray(ex, BF16))


def kernel(x, norm_g, ffn_w_gu, ffn_w_down, a_w_in, a_conv_w, a_conv_b, a_w_ra, a_b_ra, a_w_ix, a_b_ix,
           a_lambda, a_w_out, kv_norm_g, kv_w, cmp_pos_k, cmp_w1_k, cmp_w2_k, cmp_pos_v, cmp_w1_v,
           cmp_w2_v, b_w_q, b_gate_bias, b_w_o, final_g):
    nb, s, d = x.shape
    n_a, n_b = a_w_in.shape[0], b_w_q.shape[0]
    depth = n_a + n_b
    assert n_a >= 1 and norm_g.shape[0] == depth

    h = x
    for layer in range(n_a):
        hm = _rglru_mixer(h, norm_g[layer, 0], a_w_in[layer], a_conv_w[layer], a_conv_b[layer],
                          a_w_ra[layer], a_b_ra[layer], a_w_ix[layer], a_b_ix[layer], a_lambda[layer],
                          a_w_out[layer])
        h = _ffn(hm.reshape(nb * s, d), norm_g[layer, 1], ffn_w_gu[layer], ffn_w_down[layer],
                 final_g=final_g if layer == depth - 1 else None).reshape(nb, s, d)
    if n_b == 0:
        return h
    h = h.reshape(nb * s, d)

    tables = _rope_tables(s)
    xk, xv, ks, vs, kw, vw = _kv_proj(h, kv_norm_g, kv_w, tables, nb=nb, s=s)
    nchunk = s // CMP_STRIDE
    w2k = jnp.pad(cmp_w2_k, ((0, 0), (AUG, 0)))
    w2v = jnp.tile(cmp_w2_v, (1, SLOT // D_V))
    kc = _compress(xk.reshape(-1, xk.shape[-1]), cmp_pos_k, cmp_w1_k, w2k, nchunk=nchunk)
    vc = _compress(xv.reshape(-1, xv.shape[-1]), cmp_pos_v, cmp_w1_v, w2v, nchunk=nchunk)

    for j in range(n_b):
        layer = n_a + j
        qn, qr, gates = _q_proj(h, norm_g[layer, 0], b_w_q[j], b_gate_bias[j], tables, s=s)
        attn = _nsa_attn(qn, qr, gates, kc, vc, ks, vs, kw, vw, nb=nb, s=s)
        h = _ffn(h, norm_g[layer, 1], ffn_w_gu[layer], ffn_w_down[layer], attn=attn, w_o=b_w_o[j],
                 final_g=final_g if layer == depth - 1 else None)
    return h.reshape(nb, s, d)
```

```python
import functools
import math

import numpy as np
import jax
import jax.numpy as jnp
from jax import lax
from jax.experimental import pallas as pl
from jax.experimental.pallas import tpu as pltpu

CONV_WIDTH = 4
RGLRU_C = 8.0
N_HEADS = 16
N_KV_GROUPS = 4
HPG = N_HEADS // N_KV_GROUPS
D_QK = 96
D_V = 64
ROPE_DIMS = D_QK // 4
ROPE_HALF = ROPE_DIMS // 2
ROPE_THETA = 500000.0
CMP_BLOCK = 32
CMP_STRIDE = 16
SEL_BLOCK = 64
SEL_TOPK = 8
WINDOW = 512
Q_BLOCK = 128
N_BRANCH = 3
EPS = 1e-6

LANES = 128
MXU_TILE = 256
SLOT = 128
AUG = SLOT - D_QK
VMEM_LIMIT = 56 * 1024 * 1024
NEG = -1e30
SEL_BIAS = -1e9
LOG2E = math.log2(math.e)
UNROLL = 4
PRE_BLOCKS = 2

F32 = jnp.float32
BF16 = jnp.bfloat16


def _cparams(sem):
    return pltpu.CompilerParams(dimension_semantics=sem, vmem_limit_bytes=VMEM_LIMIT)


def _const_spec(shape):
    nd = len(shape)
    return pl.BlockSpec(shape, lambda *_: (0,) * nd, pipeline_mode=pl.Buffered(1))


def _rms(x, g):
    return x * lax.rsqrt(jnp.mean(x * x, axis=-1, keepdims=True) + EPS) * g


def _gelu(x):
    k1 = -2.0 * math.sqrt(2.0 / math.pi) * LOG2E
    return x / (1.0 + jnp.exp2(x * (k1 + (k1 * 0.044715) * (x * x))))


def _dot(a, b):
    return jnp.dot(a, b, preferred_element_type=F32)


def _dot_nt(a, b):
    return lax.dot_general(a, b, (((1,), (1,)), ((), ())), preferred_element_type=F32)


def _rglru_kernel(x_hbm, g_ref, win_ref, cw_ref, cb_ref, wg_ref, bra_ref, bix_ref, lam_ref, wout_ref,
                  o_hbm, xbe_ref, h_ref, ry_ref, xt_ref, ot_ref, sem_in, sem_out, *, ts, nb, cw, nch, nsteps):
    rows = ts * nb
    halo = (CONV_WIDTH - 1) * nb
    d = xt_ref.shape[-1]
    i = pl.program_id(0)
    slot = i % 2

    def in_copies(step, sl):
        return [pltpu.make_async_copy(x_hbm.at[:, step * ts + t, :], xt_ref.at[sl, t], sem_in.at[sl])
                for t in range(ts)]

    def out_copies(step, sl):
        return [pltpu.make_async_copy(ot_ref.at[sl, t], o_hbm.at[:, step * ts + t, :], sem_out.at[sl])
                for t in range(ts)]

    @pl.when(i == 0)
    def _():
        xbe_ref[:, 0:halo, :] = jnp.zeros((nch, halo, cw), F32)
        h_ref[...] = jnp.zeros_like(h_ref)
        for cp in in_copies(0, 0):
            cp.start()

    @pl.when(i + 1 < nsteps)
    def _():
        for cp in in_copies(i + 1, 1 - slot):
            cp.start()

    for cp in in_copies(i, slot):
        cp.wait()

    hn = _rms(xt_ref[slot].reshape(rows, d), g_ref[...]).astype(BF16)
    for c in range(nch):
        u = _dot(hn, win_ref[c])
        xbe_ref[c, halo:halo + rows, :] = u[:, :cw]
        y = _gelu(u[:, cw:])
        xc = cb_ref[c]
        for j in range(CONV_WIDTH):
            xc = xc + cw_ref[c, j:j + 1, :] * xbe_ref[c, j * nb:j * nb + rows, :]
        xbe_ref[c, 0:halo, :] = xbe_ref[c, rows:rows + halo, :]
        gt = _dot(xc.astype(BF16), wg_ref[c])
        r = jax.nn.sigmoid(gt[:, :cw] + bra_ref[c])
        ig = jax.nn.sigmoid(gt[:, cw:] + bix_ref[c])
        nla = r * (RGLRU_C * jax.nn.softplus(-lam_ref[c]))
        a = jnp.exp2(nla * (-LOG2E))
        om = jnp.tanh(nla) * (1.0 + a * a)
        bx = jnp.where(om > 0, om * lax.rsqrt(om), 0.0) * (ig * xc)
        h = h_ref[c]
        for t in range(ts):
            sl = slice(t * nb, (t + 1) * nb)
            h = a[sl] * h + bx[sl]
            ry_ref[sl, c * cw:(c + 1) * cw] = (h * y[sl]).astype(BF16)
        h_ref[c] = h
    out = _dot(ry_ref[...], wout_ref[...]) + xt_ref[slot].reshape(rows, d)

    @pl.when(i >= 2)
    def _():
        for cp in out_copies(i - 2, slot):
            cp.wait()

    ot_ref[slot] = out.reshape(ts, nb, d)
    for cp in out_copies(i, slot):
        cp.start()

    @pl.when(i == nsteps - 1)
    def _():
        for cp in out_copies(i, slot):
            cp.wait()
        if nsteps > 1:
            for cp in out_copies(i - 1, 1 - slot):
                cp.wait()


def _rglru_mixer(x, g, w_in, conv_w, conv_b, w_ra, b_ra, w_ix, b_ix, lam, w_out):
    nb, s, _ = x.shape
    d = w_in.shape[0]
    r = w_in.shape[1] // 2
    nblk, rb = w_ra.shape[0], w_ra.shape[1]
    per = LANES // math.gcd(rb, LANES)
    cw = per * rb
    nch = nblk // per
    assert nch * per == nblk and nch * cw == r
    ts = 16
    assert s % ts == 0 and nb % 16 == 0
    rows = ts * nb

    def chunks(v):
        return v.reshape(v.shape[0], nch, cw).transpose(1, 0, 2)

    def blockdiag(w):
        w = w.reshape(nch, per, rb, rb)
        return jnp.einsum('cpij,pq->cpiqj', w, jnp.eye(per, dtype=w.dtype)).reshape(nch, cw, cw)

    win = jnp.concatenate([chunks(w_in[:, :r]), chunks(w_in[:, r:])], axis=-1).astype(BF16)
    wg = jnp.concatenate([blockdiag(w_ra), blockdiag(w_ix)], axis=-1).astype(BF16)
    args = (x, g.reshape(1, d), win, chunks(conv_w), chunks(conv_b.reshape(1, r)), wg,
            chunks(b_ra.reshape(1, r)), chunks(b_ix.reshape(1, r)), chunks(lam.reshape(1, r)),
            w_out.astype(BF16))
    hbm = pl.BlockSpec(memory_space=pl.ANY)
    in_specs = [hbm] + [_const_spec(a.shape) for a in args[1:]]
    kern = functools.partial(_rglru_kernel, ts=ts, nb=nb, cw=cw, nch=nch, nsteps=s // ts)
    return pl.pallas_call(
        kern,
        grid=(s // ts,),
        in_specs=in_specs,
        out_specs=hbm,
        out_shape=jax.ShapeDtypeStruct((nb, s, d), F32),
        scratch_shapes=[
            pltpu.VMEM((nch, rows + (CONV_WIDTH - 1) * nb, cw), F32),
            pltpu.VMEM((nch, nb, cw), F32),
            pltpu.VMEM((rows, r), BF16),
            pltpu.VMEM((2, ts, nb, d), F32),
            pltpu.VMEM((2, ts, nb, d), F32),
            pltpu.SemaphoreType.DMA((2,)),
            pltpu.SemaphoreType.DMA((2,)),
        ],
        compiler_params=_cparams(("arbitrary",)),
        name="rglru_mixer",
    )(*args)


def _ffn_kernel(*refs, f, bounds, pre_attn, final_norm):
    it = iter(refs)
    h_ref = next(it)
    if pre_attn:
        attn_ref, wo_ref = next(it), next(it)
    g_ref, wgu_ref, wdn_ref = next(it), next(it), next(it)
    if final_norm:
        gf_ref = next(it)
    o_ref = next(it)

    h = h_ref[...]
    if pre_attn:
        h = h + _dot(attn_ref[...], wo_ref[...])
    hn = _rms(h, g_ref[...]).astype(BF16)
    acc = h
    for lo, hi in zip(bounds[:-1], bounds[1:]):
        gate = _dot(hn, wgu_ref[:, lo:hi])
        val = _dot(hn, wgu_ref[:, f + lo:f + hi])
        act = (jax.nn.silu(gate) * val).astype(BF16)
        acc = acc + _dot(act, wdn_ref[lo:hi, :])
    if final_norm:
        acc = _rms(acc, gf_ref[...])
    o_ref[...] = acc


def _ffn(h, g, w_gu, w_down, *, attn=None, w_o=None, final_g=None):
    n, d = h.shape
    f = w_down.shape[0]
    tr = 512
    mid = -(-f // (2 * MXU_TILE)) * MXU_TILE
    bounds = (0, mid, f) if mid < f else (0, f)
    assert n % tr == 0 and f % LANES == 0
    pre_attn = attn is not None
    final_norm = final_g is not None
    row_spec = lambda w: pl.BlockSpec((tr, w), lambda i: (i, 0))
    args, specs = [h], [row_spec(d)]
    if pre_attn:
        args += [attn, w_o.astype(BF16)]
        specs += [row_spec(attn.shape[1]), _const_spec(w_o.shape)]
    args += [g.reshape(1, d), w_gu.astype(BF16), w_down.astype(BF16)]
    specs += [_const_spec((1, d)), _const_spec(w_gu.shape), _const_spec(w_down.shape)]
    if final_norm:
        args.append(final_g.reshape(1, d))
        specs.append(_const_spec((1, d)))
    kern = functools.partial(_ffn_kernel, f=f, bounds=bounds, pre_attn=pre_attn, final_norm=final_norm)
    return pl.pallas_call(
        kern, grid=(n // tr,), in_specs=specs, out_specs=row_spec(d),
        out_shape=jax.ShapeDtypeStruct((n, d), F32),
        compiler_params=_cparams(("parallel",)), name="ffn",
    )(*args)


def _rope_slot(x, cos, sin_lo, sin_hi):
    return (x * cos + pltpu.roll(x, SLOT - ROPE_HALF, 1) * sin_lo
            + pltpu.roll(x, ROPE_HALF, 1) * sin_hi)


def _kv_proj_kernel(h_ref, g_ref, w_ref, cos_ref, slo_ref, shi_ref,
                    xk_ref, xv_ref, ks_ref, vs_ref, kw_ref, vw_ref, sc_ref, *, tr, s):
    g_ = N_KV_GROUPS
    hn = _rms(h_ref[...], g_ref[...]).astype(BF16)
    u = _dot(hn, w_ref[...])
    slot = lambda i: u[:, i * SLOT:(i + 1) * SLOT]
    cos, slo, shi = cos_ref[...], slo_ref[...], shi_ref[...]
    pos0 = (pl.program_id(0) % (s // tr)) * tr
    pos = pos0 + lax.broadcasted_iota(jnp.int32, (tr, SLOT), 0)
    lane = lax.broadcasted_iota(jnp.int32, (tr, SLOT), 1)
    onehot = (lane == pos // SEL_BLOCK).astype(F32)

    nj = tr // CMP_STRIDE
    for i in range(2 * g_):
        sc_ref[i] = slot(i)
    for x_ref, base in ((xk_ref, 0), (xv_ref, g_)):
        for gi in range(g_):
            for l in range(CMP_STRIDE):
                x_ref[gi, :, l * SLOT:(l + 1) * SLOT] = sc_ref[base + gi, pl.ds(l, nj, stride=CMP_STRIDE), :]

    vbase = 4 * g_
    for k_ref, v_ref, base, aug in ((ks_ref, vs_ref, 2 * g_, True), (kw_ref, vw_ref, 3 * g_, False)):
        for gi in range(g_):
            k = _rope_slot(slot(base + gi), cos, slo, shi)
            if aug:
                k = k + onehot
            k_ref[:, gi * SLOT:(gi + 1) * SLOT] = k.astype(BF16)
        for pair in range(g_ // 2):
            x = slot(vbase + pair)
            y = pltpu.roll(x, D_V, 1)
            v_ref[:, 2 * pair * SLOT:(2 * pair + 1) * SLOT] = jnp.where(lane < D_V, x, y).astype(BF16)
            v_ref[:, (2 * pair + 1) * SLOT:(2 * pair + 2) * SLOT] = jnp.where(lane < D_V, y, x).astype(BF16)
        vbase += g_ // 2


def _q_proj_kernel(h_ref, g_ref, w_ref, gb_ref, cos_ref, slo_ref, shi_ref, qn_ref, qr_ref, gt_ref):
    hn = _rms(h_ref[...], g_ref[...]).astype(BF16)
    u = _dot(hn, w_ref[...])
    cos, slo, shi = cos_ref[...], slo_ref[...], shi_ref[...]
    scale = D_QK ** -0.5 * LOG2E
    for hd in range(N_HEADS):
        q = u[:, hd * SLOT:(hd + 1) * SLOT] * scale
        qn_ref[:, hd * SLOT:(hd + 1) * SLOT] = q.astype(BF16)
        qr_ref[:, hd * SLOT:(hd + 1) * SLOT] = _rope_slot(q, cos, slo, shi).astype(BF16)
    gt_ref[...] = jax.nn.sigmoid(u[:, N_HEADS * SLOT:] + gb_ref[...])


def _pad_slots(w, n, width, left):
    d = w.shape[0]
    w = w.reshape(d, n, width)
    return jnp.pad(w, ((0, 0), (0, 0), (left, SLOT - width - left))).reshape(d, n * SLOT)


def _rope_tables(s):
    inv = ROPE_THETA ** (-jnp.arange(ROPE_HALF, dtype=F32) * 2.0 / ROPE_DIMS)
    ang = jnp.arange(s, dtype=F32)[:, None] * inv[None, :]
    cos, sin = jnp.cos(ang), jnp.sin(ang)
    z = lambda n: jnp.zeros((s, n), F32)
    o = lambda n: jnp.ones((s, n), F32)
    rest = SLOT - AUG - ROPE_DIMS
    cos_t = jnp.concatenate([o(AUG), cos, cos, o(rest)], axis=1)
    sin_lo = jnp.concatenate([z(AUG), -sin, z(ROPE_HALF), z(rest)], axis=1)
    sin_hi = jnp.concatenate([z(AUG), z(ROPE_HALF), sin, z(rest)], axis=1)
    return cos_t, sin_lo, sin_hi


def _kv_proj(h, g, kv_w, tables, *, nb, s):
    n, d = h.shape
    g_, dk, dv = N_KV_GROUPS, D_QK, D_V
    tr = 1024
    assert n % tr == 0 and s % tr == 0 and tr % CMP_STRIDE == 0
    sizes = [g_ * dk, g_ * dv] * 3
    offs = np.concatenate([[0], np.cumsum(sizes)])
    kc, vc, ksl, vsl, kwn, vwn = [kv_w[:, offs[j]:offs[j + 1]] for j in range(6)]
    assert 2 * dv == SLOT and g_ % 2 == 0
    w = jnp.concatenate([_pad_slots(kc, g_, dk, 0), _pad_slots(vc, g_, dv, 0),
                         _pad_slots(ksl, g_, dk, AUG), _pad_slots(kwn, g_, dk, AUG), vsl, vwn],
                        axis=1).astype(BF16)
    row = lambda wd, dt: (pl.BlockSpec((tr, wd), lambda i: (i, 0)), jax.ShapeDtypeStruct((n, wd), dt))
    nj = tr // CMP_STRIDE
    spt = s // tr
    xcmp = (pl.BlockSpec((None, g_, nj, CMP_STRIDE * SLOT), lambda i: (i // spt, 0, i % spt, 0)),
            jax.ShapeDtypeStruct((nb, g_, s // CMP_STRIDE, CMP_STRIDE * SLOT), F32))
    outs = [xcmp, xcmp, row(g_ * SLOT, BF16), row(g_ * SLOT, BF16), row(g_ * SLOT, BF16), row(g_ * SLOT, BF16)]
    tab_spec = pl.BlockSpec((tr, SLOT), lambda i: (i % spt, 0))
    return pl.pallas_call(
        functools.partial(_kv_proj_kernel, tr=tr, s=s),
        grid=(n // tr,),
        in_specs=[pl.BlockSpec((tr, d), lambda i: (i, 0)), _const_spec((1, d)), _const_spec(w.shape),
                  tab_spec, tab_spec, tab_spec],
        out_specs=[o[0] for o in outs], out_shape=[o[1] for o in outs],
        scratch_shapes=[pltpu.VMEM((2 * g_, tr, SLOT), F32)],
        compiler_params=_cparams(("parallel",)), name="kv_proj",
    )(h, g.reshape(1, d), w, *tables)


def _q_proj(h, g, w_q, gate_bias, tables, *, s):
    n, d = h.shape
    tr = 1024
    nq = N_HEADS * D_QK
    ngate = N_HEADS * N_BRANCH
    assert ngate <= LANES
    w = jnp.concatenate([_pad_slots(w_q[:, :nq], N_HEADS, D_QK, AUG),
                         jnp.pad(w_q[:, nq:], ((0, 0), (0, LANES - ngate)))], axis=1).astype(BF16)
    gb = jnp.pad(gate_bias.reshape(1, ngate), ((0, 0), (0, LANES - ngate)))
    tab_spec = pl.BlockSpec((tr, SLOT), lambda i: (i % (s // tr), 0))
    row = lambda wd, dt: (pl.BlockSpec((tr, wd), lambda i: (i, 0)), jax.ShapeDtypeStruct((n, wd), dt))
    outs = [row(N_HEADS * SLOT, BF16), row(N_HEADS * SLOT, BF16), row(LANES, F32)]
    return pl.pallas_call(
        _q_proj_kernel,
        grid=(n // tr,),
        in_specs=[pl.BlockSpec((tr, d), lambda i: (i, 0)), _const_spec((1, d)), _const_spec(w.shape),
                  _const_spec((1, LANES)), tab_spec, tab_spec, tab_spec],
        out_specs=[o[0] for o in outs], out_shape=[o[1] for o in outs],
        compiler_params=_cparams(("parallel",)), name="q_proj",
    )(h, g.reshape(1, d), w, gb, *tables)


def _compress_kernel(x_ref, plo_ref, phi_ref, w1lo_ref, w1hi_ref, w2_ref, o_ref):
    x = x_ref[...]
    lo = _dot((x + plo_ref[...]).astype(BF16), w1lo_ref[...])
    hi = _dot((x + phi_ref[...]).astype(BF16), w1hi_ref[...])
    pre = lo + pltpu.roll(hi, hi.shape[0] - 1, 0)
    o_ref[...] = _dot(_gelu(pre).astype(BF16), w2_ref[...]).astype(BF16)


def _compress(x, pos, w1, w2_slot, *, nchunk):
    n, width = x.shape
    dh = pos.shape[1]
    hid = w1.shape[1]
    assert CMP_BLOCK == 2 * CMP_STRIDE and width == CMP_STRIDE * SLOT
    tr = 4 * nchunk
    assert n % tr == 0

    def slots(v):
        k = v.shape[1]
        v = jnp.pad(v.reshape(2, CMP_STRIDE, dh, k), ((0, 0), (0, 0), (0, SLOT - dh), (0, 0)))
        return v.reshape(2, width, k)

    p = slots(pos.reshape(CMP_BLOCK * dh, 1))[:, :, 0].reshape(2, 1, width)
    w1s = slots(w1).astype(BF16)
    return pl.pallas_call(
        _compress_kernel,
        grid=(n // tr,),
        in_specs=[pl.BlockSpec((tr, width), lambda i: (i, 0)), _const_spec((1, width)), _const_spec((1, width)),
                  _const_spec((width, hid)), _const_spec((width, hid)), _const_spec((hid, SLOT))],
        out_specs=pl.BlockSpec((tr, SLOT), lambda i: (i, 0)),
        out_shape=jax.ShapeDtypeStruct((n, SLOT), BF16),
        compiler_params=_cparams(("parallel",)), name="compress",
    )(x, p[0], p[1], w1s[0], w1s[1], w2_slot.astype(BF16))


def _split3(x):
    hi = x.astype(BF16)
    r1 = x - hi.astype(F32)
    mid = r1.astype(BF16)
    lo = (r1 - mid.astype(F32)).astype(BF16)
    return hi, mid, lo


def _nsa_kernel(qn_ref, qr_ref, gt_ref, kc_ref, vc_ref, ks_ref, vs_ref, kw_ref, vw_ref, ov_ref, ex_ref,
                o_ref, s_ref, mx_ref, acc_ref, kwp_ref, vwp_ref, oc_ref, bias_ref, gx_ref, *, nc, ns, ktop):
    g_ = N_KV_GROUPS
    tq = Q_BLOCK
    rows = HPG * tq
    kch = 2 * tq
    wlen = WINDOW + tq
    qi = pl.program_id(1)
    s0 = pl.multiple_of(qi * tq, tq)
    hw = N_HEADS * D_V
    lane = lax.broadcasted_iota(jnp.int32, (tq, SLOT), 1)

    @pl.when(qi == 0)
    def _():
        pad_lane = lax.broadcasted_iota(jnp.int32, (WINDOW, g_ * SLOT), 1) & (SLOT - 1)
        kwp_ref[0:WINDOW, :] = (pad_lane == 0).astype(BF16)
        vwp_ref[0:WINDOW, :] = jnp.zeros((WINDOW, g_ * SLOT), BF16)
        kwp_ref[WINDOW:, :] = kw_ref[...]
        vwp_ref[WINDOW:, :] = vw_ref[...]

    r_row = lax.broadcasted_iota(jnp.int32, (rows, 1), 0) & (tq - 1)
    stack = lambda ref, g: [ref[:, hd * SLOT:(hd + 1) * SLOT] for hd in range(g * HPG, (g + 1) * HPG)]

    @pl.when(qi % PRE_BLOCKS == 0)
    def _():
        tp = PRE_BLOCKS * tq
        ghi, gmid, _ = _split3(gt_ref[...])
        gx_all = _dot(ghi, ex_ref[...]) + _dot(gmid, ex_ref[...])
        for blk in range(PRE_BLOCKS):
            gx_ref[blk] = gx_all[blk * tq:(blk + 1) * tq]
        ncp = kc_ref.shape[1]
        c_idx = lax.broadcasted_iota(jnp.int32, (ncp, 1), 0)
        c_end = jnp.where(c_idx < nc, c_idx * CMP_STRIDE + (CMP_BLOCK - 1), jnp.iinfo(jnp.int32).max)
        t_col = s0 + (lax.broadcasted_iota(jnp.int32, (1, HPG * tp), 1) & (tp - 1))
        valid = c_end <= t_col
        psums = []
        for g in range(g_):
            st = _dot_nt(kc_ref[g], jnp.concatenate(stack(qn_ref, g), axis=0))
            st = jnp.where(valid, st, NEG)
            e = jnp.exp2(st - jnp.maximum(jnp.max(st, axis=0, keepdims=True), 0.1 * NEG))
            den = jnp.sum(e, axis=0, keepdims=True)
            pt = e * (1.0 / jnp.where(den > 0, den, 1.0))
            o_all = _dot(pt.astype(BF16).T, vc_ref[g])
            for blk in range(PRE_BLOCKS):
                oc_ref[blk, g] = jnp.concatenate(
                    [o_all[hh * tp + blk * tq:hh * tp + (blk + 1) * tq] for hh in range(HPG)], axis=0)
            psum = pt[:, 0:tp]
            for i in range(1, HPG):
                psum = psum + pt[:, i * tp:(i + 1) * tp]
            psums.append(psum)

        psum = jnp.concatenate(psums, axis=1)
        imp = sum(_dot(ov_ref[...], part) for part in _split3(psum))[0:ns, :]
        n_i = lax.broadcasted_iota(jnp.int32, imp.shape, 0)
        tt = s0 + (lax.broadcasted_iota(jnp.int32, imp.shape, 1) & (tp - 1))
        cur = tt // SEL_BLOCK
        sel = (n_i == 0) | (n_i == cur) | (n_i == cur - 1)
        score = jnp.where(sel | (n_i * SEL_BLOCK > tt), -jnp.inf, imp)
        for _ in range(ktop - 3):
            mx = jnp.max(score, axis=0, keepdims=True)
            idx = jnp.min(jnp.where(score == mx, n_i, ns), axis=0, keepdims=True)
            hit = n_i == idx
            sel = sel | hit
            score = jnp.where(hit, -jnp.inf, score)
        bias_t = jnp.where(sel, 0.0, SEL_BIAS).astype(BF16)
        bias_t = jnp.concatenate([bias_t, jnp.zeros((SLOT - ns, g_ * tp), BF16)], axis=0)
        for g in range(g_):
            for blk in range(PRE_BLOCKS):
                c0 = g * tp + blk * tq
                bias_ref[blk, g] = bias_t[:, c0:c0 + tq].T

    pre = qi % PRE_BLOCKS
    qa = []
    for g in range(g_):
        bias = bias_ref[pre, g]
        qa.append(jnp.concatenate([q + bias for q in stack(qr_ref, g)], axis=0))

    nchunk = (qi + 2) // 2
    dif = lax.broadcasted_iota(jnp.int32, (rows, kch), 1) - r_row

    def run_chunks(n, fn):
        def body(q, carry):
            fn(UNROLL * q, UNROLL)
            return carry

        lax.fori_loop(0, n // UNROLL, body, 0)
        for rem in range(1, UNROLL):
            @pl.when(n % UNROLL == rem)
            def _(rem=rem):
                fn((n // UNROLL) * UNROLL, rem)

    def score_chunks(j, nk):
        for g in range(g_):
            mg = mx_ref[g]
            for i in range(nk):
                k0 = pl.multiple_of((j + i) * kch, kch)
                sc = _dot_nt(qa[g], ks_ref[pl.ds(k0, kch), g * SLOT:(g + 1) * SLOT])
                sc = jnp.where(dif <= s0 - k0, sc, NEG)
                s_ref[g, j + i] = sc
                mg = jnp.maximum(mg, jnp.maximum(sc[:, :SLOT], sc[:, SLOT:]))
            mx_ref[g] = mg

    mx_ref[...] = jnp.full(mx_ref.shape, NEG, F32)
    run_chunks(nchunk, score_chunks)
    for g in range(g_):
        mx_ref[g] = jnp.broadcast_to(jnp.max(mx_ref[g], axis=1, keepdims=True), (rows, SLOT))
    acc_ref[...] = jnp.zeros(acc_ref.shape, F32)

    def pv_chunks(j, nk):
        k0 = pl.multiple_of(j * kch, kch)
        ones_k = jnp.ones((nk * kch, SLOT), BF16)
        for g in range(g_):
            m = mx_ref[g]
            parts = []
            for i in range(nk):
                sc = s_ref[g, j + i]
                parts += [jnp.exp2(sc[:, :SLOT] - m), jnp.exp2(sc[:, SLOT:] - m)]
            p = jnp.concatenate(parts, axis=1).astype(BF16)
            v = jnp.concatenate([vs_ref[pl.ds(k0, nk * kch), g * SLOT:(g + 1) * SLOT], ones_k], axis=1)
            acc_ref[g] += _dot(p, v)

    run_chunks(nchunk, pv_chunks)

    ones_w = jnp.ones((wlen, SLOT), BF16)
    nwt = wlen // tq
    wdif = lax.broadcasted_iota(jnp.int32, (rows, tq), 1) - r_row
    pad_bias = jnp.where(lane == 0, SEL_BIAS, 0.0).astype(BF16)
    o_ws = []
    for g in range(g_):
        qw = jnp.concatenate([q + pad_bias for q in stack(qr_ref, g)], axis=0)
        sw = _dot_nt(qw, kwp_ref[pl.ds(s0, wlen), g * SLOT:(g + 1) * SLOT])
        tiles = [sw[:, i * tq:(i + 1) * tq] for i in range(nwt)]
        tiles[0] = jnp.where(wdif > 0, tiles[0], NEG)
        tiles[-1] = jnp.where(wdif <= 0, tiles[-1], NEG)
        mw = tiles[0]
        for t in tiles[1:]:
            mw = jnp.maximum(mw, t)
        mw = jnp.max(mw, axis=1, keepdims=True)
        pw = jnp.concatenate([jnp.exp2(t - mw) for t in tiles], axis=1).astype(BF16)
        vwin = jnp.concatenate([vwp_ref[pl.ds(s0, wlen), g * SLOT:(g + 1) * SLOT], ones_w], axis=1)
        accw = _dot(pw, vwin)
        o_ws.append(accw[:, :SLOT] / accw[:, SLOT:])

    for g in range(g_):
        acc = acc_ref[g]
        o_s = acc[:, :SLOT] / acc[:, SLOT:]
        o_w = o_ws[g]
        for hp in range(HPG // 2):
            col = (g * HPG + 2 * hp) * D_V
            ev, od = slice(2 * hp * tq, (2 * hp + 1) * tq), slice((2 * hp + 1) * tq, (2 * hp + 2) * tq)
            tile = jnp.zeros((tq, SLOT), F32)
            for br, o_b in enumerate((oc_ref[pre, g], o_s, o_w)):
                pair = jnp.where(lane < D_V, o_b[ev], o_b[od])
                tile = tile + gx_ref[pre, :, br * hw + col:br * hw + col + SLOT] * pair
            o_ref[:, col:col + SLOT] = tile.astype(BF16)


def _nsa_attn(qn, qr, gates, kc, vc, ks, vs, kw, vw, *, nb, s):
    tq = Q_BLOCK
    nq = s // tq
    nc = (s - CMP_BLOCK) // CMP_STRIDE + 1
    ncp = s // CMP_STRIDE
    ns = s // SEL_BLOCK
    ktop = min(SEL_TOPK, ns)
    g_ = N_KV_GROUPS
    assert ncp % 8 == 0 and ns <= AUG and ktop >= 3 and s % (2 * tq) == 0 and 2 * D_V == SLOT
    assert nq % PRE_BLOCKS == 0
    hw = N_HEADS * D_V
    cs = np.arange(ncp) * CMP_STRIDE
    ss = np.arange(SLOT) * SEL_BLOCK
    ov = ((cs[None, :] < ss[:, None] + SEL_BLOCK) & (cs[None, :] + CMP_BLOCK > ss[:, None])
          & (np.arange(ncp)[None, :] < nc) & (np.arange(SLOT)[:, None] < ns))
    ex = np.zeros((LANES, N_BRANCH * hw), np.float32)
    for hd in range(N_HEADS):
        for br in range(N_BRANCH):
            ex[hd * N_BRANCH + br, br * hw + hd * D_V:br * hw + (hd + 1) * D_V] = 1.0
    qspec = lambda wd: pl.BlockSpec((tq, wd), lambda b, q: (b * nq + q, 0))
    pspec = lambda wd: pl.BlockSpec((PRE_BLOCKS * tq, wd), lambda b, q: ((b * nq + q) // PRE_BLOCKS, 0))
    seq = lambda wd: pl.BlockSpec((s, wd), lambda b, q: (b, 0))
    cmp_spec = pl.BlockSpec((None, g_, ncp, SLOT), lambda b, q: (b, 0, 0, 0))
    nkc = s // (2 * tq)
    return pl.pallas_call(
        functools.partial(_nsa_kernel, nc=nc, ns=ns, ktop=ktop),
        grid=(nb, nq),
        in_specs=[pspec(N_HEADS * SLOT), qspec(N_HEADS * SLOT), pspec(LANES), cmp_spec, cmp_spec,
                  seq(g_ * SLOT), seq(g_ * SLOT), seq(g_ * SLOT), seq(g_ * SLOT),
                  _const_spec(ov.shape), _const_spec(ex.shape)],
        out_specs=qspec(hw),
        out_shape=jax.ShapeDtypeStruct((nb * s, hw), BF16),
        scratch_shapes=[pltpu.VMEM((g_, nkc, HPG * tq, 2 * tq), F32),
                        pltpu.VMEM((g_, HPG * tq, SLOT), F32),
                        pltpu.VMEM((g_, HPG * tq, 2 * SLOT), F32),
                        pltpu.VMEM((s + WINDOW, g_ * SLOT), BF16),
                        pltpu.VMEM((s + WINDOW, g_ * SLOT), BF16),
                        pltpu.VMEM((PRE_BLOCKS, g_, HPG * tq, SLOT), F32),
                        pltpu.VMEM((PRE_BLOCKS, g_, tq, SLOT), BF16),
                        pltpu.VMEM((PRE_BLOCKS, tq, N_BRANCH * hw), F32)],
        compiler_params=_cparams(("parallel", "arbitrary")), name="nsa_attn",
    )(qn, qr, gates, kc.reshape(nb, g_, ncp, SLOT), vc.reshape(nb, g_, ncp, SLOT), ks, vs, kw, vw,
      jnp.asarray(ov, BF16), jnp.asarray(ex, BF16))


def kernel(x, norm_g, ffn_w_gu, ffn_w_down, a_w_in, a_conv_w, a_conv_b, a_w_ra, a_b_ra, a_w_ix, a_b_ix,
           a_lambda, a_w_out, kv_norm_g, kv_w, cmp_pos_k, cmp_w1_k, cmp_w2_k, cmp_pos_v, cmp_w1_v,
           cmp_w2_v, b_w_q, b_gate_bias, b_w_o, final_g):
    nb, s, d = x.shape
    n_a, n_b = a_w_in.shape[0], b_w_q.shape[0]
    depth = n_a + n_b
    assert n_a >= 1 and norm_g.shape[0] == depth

    h = x
    for layer in range(n_a):
        hm = _rglru_mixer(h, norm_g[layer, 0], a_w_in[layer], a_conv_w[layer], a_conv_b[layer],
                          a_w_ra[layer], a_b_ra[layer], a_w_ix[layer], a_b_ix[layer], a_lambda[layer],
                          a_w_out[layer])
        h = _ffn(hm.reshape(nb * s, d), norm_g[layer, 1], ffn_w_gu[layer], ffn_w_down[layer],
                 final_g=final_g if layer == depth - 1 else None).reshape(nb, s, d)
    if n_b == 0:
        return h
    h = h.reshape(nb * s, d)

    tables = _rope_tables(s)
    xk, xv, ks, vs, kw, vw = _kv_proj(h, kv_norm_g, kv_w, tables, nb=nb, s=s)
    nchunk = s // CMP_STRIDE
    w2k = jnp.pad(cmp_w2_k, ((0, 0), (AUG, 0)))
    w2v = jnp.tile(cmp_w2_v, (1, SLOT // D_V))
    kc = _compress(xk.reshape(-1, xk.shape[-1]), cmp_pos_k, cmp_w1_k, w2k, nchunk=nchunk)
    vc = _compress(xv.reshape(-1, xv.shape[-1]), cmp_pos_v, cmp_w1_v, w2v, nchunk=nchunk)

    for j in range(n_b):
        layer = n_a + j
        qn, qr, gates = _q_proj(h, norm_g[layer, 0], b_w_q[j], b_gate_bias[j], tables, s=s)
        attn = _nsa_attn(qn, qr, gates, kc, vc, ks, vs, kw, vw, nb=nb, s=s)
        h = _ffn(h, norm_g[layer, 1], ffn_w_gu[layer], ffn_w_down[layer], attn=attn, w_o=b_w_o[j],
                 final_g=final_g if layer == depth - 1 else None)
    return h.reshape(nb, s, d)
```

```python
import functools
import math

import numpy as np
import jax
import jax.numpy as jnp
from jax import lax
from jax.experimental import pallas as pl
from jax.experimental.pallas import tpu as pltpu

CONV_WIDTH = 4
RGLRU_C = 8.0
N_HEADS = 16
N_KV_GROUPS = 4
HPG = N_HEADS // N_KV_GROUPS
D_QK = 96
D_V = 64
ROPE_DIMS = D_QK // 4
ROPE_HALF = ROPE_DIMS // 2
ROPE_THETA = 500000.0
CMP_BLOCK = 32
CMP_STRIDE = 16
SEL_BLOCK = 64
SEL_TOPK = 8
WINDOW = 512
Q_BLOCK = 128
N_BRANCH = 3
EPS = 1e-6

LANES = 128
MXU_TILE = 256
SLOT = 128
AUG = SLOT - D_QK
VMEM_LIMIT = 56 * 1024 * 1024
NEG = -1e30
SEL_BIAS = -1e9
LOG2E = math.log2(math.e)
UNROLL = 4
PRE_BLOCKS = 2

F32 = jnp.float32
BF16 = jnp.bfloat16


def _cparams(sem):
    return pltpu.CompilerParams(dimension_semantics=sem, vmem_limit_bytes=VMEM_LIMIT)


def _const_spec(shape):
    nd = len(shape)
    return pl.BlockSpec(shape, lambda *_: (0,) * nd, pipeline_mode=pl.Buffered(1))


def _rms(x, g):
    return x * lax.rsqrt(jnp.mean(x * x, axis=-1, keepdims=True) + EPS) * g


def _gelu(x):
    k1 = -2.0 * math.sqrt(2.0 / math.pi) * LOG2E
    return x / (1.0 + jnp.exp2(x * (k1 + (k1 * 0.044715) * (x * x))))


def _dot(a, b):
    return jnp.dot(a, b, preferred_element_type=F32)


def _dot_nt(a, b):
    return lax.dot_general(a, b, (((1,), (1,)), ((), ())), preferred_element_type=F32)


def _rglru_kernel(x_hbm, g_ref, win_ref, cw_ref, cb_ref, wg_ref, bra_ref, bix_ref, lam_ref, wout_ref,
                  o_hbm, xbe_ref, h_ref, ry_ref, xt_ref, ot_ref, sem_in, sem_out, *, ts, nb, cw, nch, nsteps):
    rows = ts * nb
    halo = (CONV_WIDTH - 1) * nb
    d = xt_ref.shape[-1]
    i = pl.program_id(0)
    slot = i % 2

    def in_copies(step, sl):
        return [pltpu.make_async_copy(x_hbm.at[:, step * ts + t, :], xt_ref.at[sl, t], sem_in.at[sl])
                for t in range(ts)]

    def out_copies(step, sl):
        return [pltpu.make_async_copy(ot_ref.at[sl, t], o_hbm.at[:, step * ts + t, :], sem_out.at[sl])
                for t in range(ts)]

    @pl.when(i == 0)
    def _():
        xbe_ref[:, 0:halo, :] = jnp.zeros((nch, halo, cw), F32)
        h_ref[...] = jnp.zeros_like(h_ref)
        for cp in in_copies(0, 0):
            cp.start()

    @pl.when(i + 1 < nsteps)
    def _():
        for cp in in_copies(i + 1, 1 - slot):
            cp.start()

    for cp in in_copies(i, slot):
        cp.wait()

    hn = _rms(xt_ref[slot].reshape(rows, d), g_ref[...]).astype(BF16)
    for c in range(nch):
        u = _dot(hn, win_ref[c])
        xbe_ref[c, halo:halo + rows, :] = u[:, :cw]
        y = _gelu(u[:, cw:])
        xc = cb_ref[c]
        for j in range(CONV_WIDTH):
            xc = xc + cw_ref[c, j:j + 1, :] * xbe_ref[c, j * nb:j * nb + rows, :]
        xbe_ref[c, 0:halo, :] = xbe_ref[c, rows:rows + halo, :]
        gt = _dot(xc.astype(BF16), wg_ref[c])
        r = jax.nn.sigmoid(gt[:, :cw] + bra_ref[c])
        ig = jax.nn.sigmoid(gt[:, cw:] + bix_ref[c])
        nla = r * (RGLRU_C * jax.nn.softplus(-lam_ref[c]))
        a = jnp.exp2(nla * (-LOG2E))
        om = jnp.tanh(nla) * (1.0 + a * a)
        bx = jnp.where(om > 0, om * lax.rsqrt(om), 0.0) * (ig * xc)
        h = h_ref[c]
        for t in range(ts):
            sl = slice(t * nb, (t + 1) * nb)
            h = a[sl] * h + bx[sl]
            ry_ref[sl, c * cw:(c + 1) * cw] = (h * y[sl]).astype(BF16)
        h_ref[c] = h
    out = _dot(ry_ref[...], wout_ref[...]) + xt_ref[slot].reshape(rows, d)

    @pl.when(i >= 2)
    def _():
        for cp in out_copies(i - 2, slot):
            cp.wait()

    ot_ref[slot] = out.reshape(ts, nb, d)
    for cp in out_copies(i, slot):
        cp.start()

    @pl.when(i == nsteps - 1)
    def _():
        for cp in out_copies(i, slot):
            cp.wait()
        if nsteps > 1:
            for cp in out_copies(i - 1, 1 - slot):
                cp.wait()


def _rglru_mixer(x, g, w_in, conv_w, conv_b, w_ra, b_ra, w_ix, b_ix, lam, w_out):
    nb, s, _ = x.shape
    d = w_in.shape[0]
    r = w_in.shape[1] // 2
    nblk, rb = w_ra.shape[0], w_ra.shape[1]
    per = LANES // math.gcd(rb, LANES)
    cw = per * rb
    nch = nblk // per
    assert nch * per == nblk and nch * cw == r
    ts = 16
    assert s % ts == 0 and nb % 16 == 0
    rows = ts * nb

    def chunks(v):
        return v.reshape(v.shape[0], nch, cw).transpose(1, 0, 2)

    def blockdiag(w):
        w = w.reshape(nch, per, rb, rb)
        return jnp.einsum('cpij,pq->cpiqj', w, jnp.eye(per, dtype=w.dtype)).reshape(nch, cw, cw)

    win = jnp.concatenate([chunks(w_in[:, :r]), chunks(w_in[:, r:])], axis=-1).astype(BF16)
    wg = jnp.concatenate([blockdiag(w_ra), blockdiag(w_ix)], axis=-1).astype(BF16)
    args = (x, g.reshape(1, d), win, chunks(conv_w), chunks(conv_b.reshape(1, r)), wg,
            chunks(b_ra.reshape(1, r)), chunks(b_ix.reshape(1, r)), chunks(lam.reshape(1, r)),
            w_out.astype(BF16))
    hbm = pl.BlockSpec(memory_space=pl.ANY)
    in_specs = [hbm] + [_const_spec(a.shape) for a in args[1:]]
    kern = functools.partial(_rglru_kernel, ts=ts, nb=nb, cw=cw, nch=nch, nsteps=s // ts)
    return pl.pallas_call(
        kern,
        grid=(s // ts,),
        in_specs=in_specs,
        out_specs=hbm,
        out_shape=jax.ShapeDtypeStruct((nb, s, d), F32),
        scratch_shapes=[
            pltpu.VMEM((nch, rows + (CONV_WIDTH - 1) * nb, cw), F32),
            pltpu.VMEM((nch, nb, cw), F32),
            pltpu.VMEM((rows, r), BF16),
            pltpu.VMEM((2, ts, nb, d), F32),
            pltpu.VMEM((2, ts, nb, d), F32),
            pltpu.SemaphoreType.DMA((2,)),
            pltpu.SemaphoreType.DMA((2,)),
        ],
        compiler_params=_cparams(("arbitrary",)),
        name="rglru_mixer",
    )(*args)


def _ffn_kernel(*refs, f, bounds, pre_attn, final_norm):
    it = iter(refs)
    h_ref = next(it)
    if pre_attn:
        attn_ref, wo_ref = next(it), next(it)
    g_ref, wgu_ref, wdn_ref = next(it), next(it), next(it)
    if final_norm:
        gf_ref = next(it)
    o_ref = next(it)

    h = h_ref[...]
    if pre_attn:
        h = h + _dot(attn_ref[...], wo_ref[...])
    hn = _rms(h, g_ref[...]).astype(BF16)
    acc = h
    for lo, hi in zip(bounds[:-1], bounds[1:]):
        gate = _dot(hn, wgu_ref[:, lo:hi])
        val = _dot(hn, wgu_ref[:, f + lo:f + hi])
        act = (jax.nn.silu(gate) * val).astype(BF16)
        acc = acc + _dot(act, wdn_ref[lo:hi, :])
    if final_norm:
        acc = _rms(acc, gf_ref[...])
    o_ref[...] = acc


def _ffn(h, g, w_gu, w_down, *, attn=None, w_o=None, final_g=None):
    n, d = h.shape
    f = w_down.shape[0]
    tr = 512
    mid = -(-f // (2 * MXU_TILE)) * MXU_TILE
    bounds = (0, mid, f) if mid < f else (0, f)
    assert n % tr == 0 and f % LANES == 0
    pre_attn = attn is not None
    final_norm = final_g is not None
    row_spec = lambda w: pl.BlockSpec((tr, w), lambda i: (i, 0))
    args, specs = [h], [row_spec(d)]
    if pre_attn:
        args += [attn, w_o.astype(BF16)]
        specs += [row_spec(attn.shape[1]), _const_spec(w_o.shape)]
    args += [g.reshape(1, d), w_gu.astype(BF16), w_down.astype(BF16)]
    specs += [_const_spec((1, d)), _const_spec(w_gu.shape), _const_spec(w_down.shape)]
    if final_norm:
        args.append(final_g.reshape(1, d))
        specs.append(_const_spec((1, d)))
    kern = functools.partial(_ffn_kernel, f=f, bounds=bounds, pre_attn=pre_attn, final_norm=final_norm)
    return pl.pallas_call(
        kern, grid=(n // tr,), in_specs=specs, out_specs=row_spec(d),
        out_shape=jax.ShapeDtypeStruct((n, d), F32),
        compiler_params=_cparams(("parallel",)), name="ffn",
    )(*args)


def _rope_slot(x, cos, sin_lo, sin_hi):
    return (x * cos + pltpu.roll(x, SLOT - ROPE_HALF, 1) * sin_lo
            + pltpu.roll(x, ROPE_HALF, 1) * sin_hi)


def _kv_proj_kernel(h_ref, g_ref, w_ref, cos_ref, slo_ref, shi_ref,
                    xk_ref, xv_ref, ks_ref, vs_ref, kw_ref, vw_ref, sc_ref, *, tr, s):
    g_ = N_KV_GROUPS
    hn = _rms(h_ref[...], g_ref[...]).astype(BF16)
    u = _dot(hn, w_ref[...])
    slot = lambda i: u[:, i * SLOT:(i + 1) * SLOT]
    cos, slo, shi = cos_ref[...], slo_ref[...], shi_ref[...]
    pos0 = (pl.program_id(0) % (s // tr)) * tr
    pos = pos0 + lax.broadcasted_iota(jnp.int32, (tr, SLOT), 0)
    lane = lax.broadcasted_iota(jnp.int32, (tr, SLOT), 1)
    onehot = (lane == pos // SEL_BLOCK).astype(F32)

    nj = tr // CMP_STRIDE
    for i in range(2 * g_):
        sc_ref[i] = slot(i)
    for x_ref, base in ((xk_ref, 0), (xv_ref, g_)):
        for gi in range(g_):
            for l in range(CMP_STRIDE):
                x_ref[gi, :, l * SLOT:(l + 1) * SLOT] = sc_ref[base + gi, pl.ds(l, nj, stride=CMP_STRIDE), :]

    vbase = 4 * g_
    for k_ref, v_ref, base, aug in ((ks_ref, vs_ref, 2 * g_, True), (kw_ref, vw_ref, 3 * g_, False)):
        for gi in range(g_):
            k = _rope_slot(slot(base + gi), cos, slo, shi)
            if aug:
                k = k + onehot
            k_ref[:, gi * SLOT:(gi + 1) * SLOT] = k.astype(BF16)
        for pair in range(g_ // 2):
            x = slot(vbase + pair)
            y = pltpu.roll(x, D_V, 1)
            v_ref[:, 2 * pair * SLOT:(2 * pair + 1) * SLOT] = jnp.where(lane < D_V, x, y).astype(BF16)
            v_ref[:, (2 * pair + 1) * SLOT:(2 * pair + 2) * SLOT] = jnp.where(lane < D_V, y, x).astype(BF16)
        vbase += g_ // 2


def _q_proj_kernel(h_ref, g_ref, w_ref, gb_ref, cos_ref, slo_ref, shi_ref, qn_ref, qr_ref, gt_ref):
    hn = _rms(h_ref[...], g_ref[...]).astype(BF16)
    u = _dot(hn, w_ref[...])
    cos, slo, shi = cos_ref[...], slo_ref[...], shi_ref[...]
    scale = D_QK ** -0.5 * LOG2E
    for hd in range(N_HEADS):
        q = u[:, hd * SLOT:(hd + 1) * SLOT] * scale
        qn_ref[:, hd * SLOT:(hd + 1) * SLOT] = q.astype(BF16)
        qr_ref[:, hd * SLOT:(hd + 1) * SLOT] = _rope_slot(q, cos, slo, shi).astype(BF16)
    gt_ref[...] = jax.nn.sigmoid(u[:, N_HEADS * SLOT:] + gb_ref[...])


def _pad_slots(w, n, width, left):
    d = w.shape[0]
    w = w.reshape(d, n, width)
    return jnp.pad(w, ((0, 0), (0, 0), (left, SLOT - width - left))).reshape(d, n * SLOT)


def _rope_tables(s):
    inv = ROPE_THETA ** (-jnp.arange(ROPE_HALF, dtype=F32) * 2.0 / ROPE_DIMS)
    ang = jnp.arange(s, dtype=F32)[:, None] * inv[None, :]
    cos, sin = jnp.cos(ang), jnp.sin(ang)
    z = lambda n: jnp.zeros((s, n), F32)
    o = lambda n: jnp.ones((s, n), F32)
    rest = SLOT - AUG - ROPE_DIMS
    cos_t = jnp.concatenate([o(AUG), cos, cos, o(rest)], axis=1)
    sin_lo = jnp.concatenate([z(AUG), -sin, z(ROPE_HALF), z(rest)], axis=1)
    sin_hi = jnp.concatenate([z(AUG), z(ROPE_HALF), sin, z(rest)], axis=1)
    return cos_t, sin_lo, sin_hi


def _kv_proj(h, g, kv_w, tables, *, nb, s):
    n, d = h.shape
    g_, dk, dv = N_KV_GROUPS, D_QK, D_V
    tr = 1024
    assert n % tr == 0 and s % tr == 0 and tr % CMP_STRIDE == 0
    sizes = [g_ * dk, g_ * dv] * 3
    offs = np.concatenate([[0], np.cumsum(sizes)])
    kc, vc, ksl, vsl, kwn, vwn = [kv_w[:, offs[j]:offs[j + 1]] for j in range(6)]
    assert 2 * dv == SLOT and g_ % 2 == 0
    w = jnp.concatenate([_pad_slots(kc, g_, dk, 0), _pad_slots(vc, g_, dv, 0),
                         _pad_slots(ksl, g_, dk, AUG), _pad_slots(kwn, g_, dk, AUG), vsl, vwn],
                        axis=1).astype(BF16)
    row = lambda wd, dt: (pl.BlockSpec((tr, wd), lambda i: (i, 0)), jax.ShapeDtypeStruct((n, wd), dt))
    nj = tr // CMP_STRIDE
    spt = s // tr
    xcmp = (pl.BlockSpec((None, g_, nj, CMP_STRIDE * SLOT), lambda i: (i // spt, 0, i % spt, 0)),
            jax.ShapeDtypeStruct((nb, g_, s // CMP_STRIDE, CMP_STRIDE * SLOT), F32))
    outs = [xcmp, xcmp, row(g_ * SLOT, BF16), row(g_ * SLOT, BF16), row(g_ * SLOT, BF16), row(g_ * SLOT, BF16)]
    tab_spec = pl.BlockSpec((tr, SLOT), lambda i: (i % spt, 0))
    return pl.pallas_call(
        functools.partial(_kv_proj_kernel, tr=tr, s=s),
        grid=(n // tr,),
        in_specs=[pl.BlockSpec((tr, d), lambda i: (i, 0)), _const_spec((1, d)), _const_spec(w.shape),
                  tab_spec, tab_spec, tab_spec],
        out_specs=[o[0] for o in outs], out_shape=[o[1] for o in outs],
        scratch_shapes=[pltpu.VMEM((2 * g_, tr, SLOT), F32)],
        compiler_params=_cparams(("parallel",)), name="kv_proj",
    )(h, g.reshape(1, d), w, *tables)


def _q_proj(h, g, w_q, gate_bias, tables, *, s):
    n, d = h.shape
    tr = 1024
    nq = N_HEADS * D_QK
    ngate = N_HEADS * N_BRANCH
    assert ngate <= LANES
    w = jnp.concatenate([_pad_slots(w_q[:, :nq], N_HEADS, D_QK, AUG),
                         jnp.pad(w_q[:, nq:], ((0, 0), (0, LANES - ngate)))], axis=1).astype(BF16)
    gb = jnp.pad(gate_bias.reshape(1, ngate), ((0, 0), (0, LANES - ngate)))
    tab_spec = pl.BlockSpec((tr, SLOT), lambda i: (i % (s // tr), 0))
    row = lambda wd, dt: (pl.BlockSpec((tr, wd), lambda i: (i, 0)), jax.ShapeDtypeStruct((n, wd), dt))
    outs = [row(N_HEADS * SLOT, BF16), row(N_HEADS * SLOT, BF16), row(LANES, F32)]
    return pl.pallas_call(
        _q_proj_kernel,
        grid=(n // tr,),
        in_specs=[pl.BlockSpec((tr, d), lambda i: (i, 0)), _const_spec((1, d)), _const_spec(w.shape),
                  _const_spec((1, LANES)), tab_spec, tab_spec, tab_spec],
        out_specs=[o[0] for o in outs], out_shape=[o[1] for o in outs],
        compiler_params=_cparams(("parallel",)), name="q_proj",
    )(h, g.reshape(1, d), w, gb, *tables)


def _compress_kernel(x_ref, plo_ref, phi_ref, w1lo_ref, w1hi_ref, w2_ref, o_ref):
    x = x_ref[...]
    lo = _dot((x + plo_ref[...]).astype(BF16), w1lo_ref[...])
    hi = _dot((x + phi_ref[...]).astype(BF16), w1hi_ref[...])
    pre = lo + pltpu.roll(hi, hi.shape[0] - 1, 0)
    o_ref[...] = _dot(_gelu(pre).astype(BF16), w2_ref[...]).astype(BF16)


def _compress(x, pos, w1, w2_slot, *, nchunk):
    n, width = x.shape
    dh = pos.shape[1]
    hid = w1.shape[1]
    assert CMP_BLOCK == 2 * CMP_STRIDE and width == CMP_STRIDE * SLOT
    tr = 4 * nchunk
    assert n % tr == 0

    def slots(v):
        k = v.shape[1]
        v = jnp.pad(v.reshape(2, CMP_STRIDE, dh, k), ((0, 0), (0, 0), (0, SLOT - dh), (0, 0)))
        return v.reshape(2, width, k)

    p = slots(pos.reshape(CMP_BLOCK * dh, 1))[:, :, 0].reshape(2, 1, width)
    w1s = slots(w1).astype(BF16)
    return pl.pallas_call(
        _compress_kernel,
        grid=(n // tr,),
        in_specs=[pl.BlockSpec((tr, width), lambda i: (i, 0)), _const_spec((1, width)), _const_spec((1, width)),
                  _const_spec((width, hid)), _const_spec((width, hid)), _const_spec((hid, SLOT))],
        out_specs=pl.BlockSpec((tr, SLOT), lambda i: (i, 0)),
        out_shape=jax.ShapeDtypeStruct((n, SLOT), BF16),
        compiler_params=_cparams(("parallel",)), name="compress",
    )(x, p[0], p[1], w1s[0], w1s[1], w2_slot.astype(BF16))


def _split3(x):
    hi = x.astype(BF16)
    r1 = x - hi.astype(F32)
    mid = r1.astype(BF16)
    lo = (r1 - mid.astype(F32)).astype(BF16)
    return hi, mid, lo


def _nsa_kernel(qn_ref, qr_ref, gt_ref, kc_ref, vc_ref, ks_ref, vs_ref, kw_ref, vw_ref, ov_ref, ex_ref,
                o_ref, mx_ref, acc_ref, kwp_ref, vwp_ref, oc_ref, bias_ref, gx_ref, *, nc, ns, ktop):
    g_ = N_KV_GROUPS
    tq = Q_BLOCK
    rows = HPG * tq
    kch = 2 * tq
    wlen = WINDOW + tq
    qi = pl.program_id(1)
    s0 = pl.multiple_of(qi * tq, tq)
    hw = N_HEADS * D_V
    lane = lax.broadcasted_iota(jnp.int32, (tq, SLOT), 1)

    @pl.when(qi == 0)
    def _():
        pad_lane = lax.broadcasted_iota(jnp.int32, (WINDOW, g_ * SLOT), 1) & (SLOT - 1)
        kwp_ref[0:WINDOW, :] = (pad_lane == 0).astype(BF16)
        vwp_ref[0:WINDOW, :] = jnp.zeros((WINDOW, g_ * SLOT), BF16)
        kwp_ref[WINDOW:, :] = kw_ref[...]
        vwp_ref[WINDOW:, :] = vw_ref[...]

    r_row = lax.broadcasted_iota(jnp.int32, (rows, 1), 0) & (tq - 1)
    stack = lambda ref, g: [ref[:, hd * SLOT:(hd + 1) * SLOT] for hd in range(g * HPG, (g + 1) * HPG)]

    @pl.when(qi % PRE_BLOCKS == 0)
    def _():
        tp = PRE_BLOCKS * tq
        ghi, gmid, _ = _split3(gt_ref[...])
        gx_all = _dot(ghi, ex_ref[...]) + _dot(gmid, ex_ref[...])
        for blk in range(PRE_BLOCKS):
            gx_ref[blk] = gx_all[blk * tq:(blk + 1) * tq]
        ncp = kc_ref.shape[1]
        c_idx = lax.broadcasted_iota(jnp.int32, (ncp, 1), 0)
        c_end = jnp.where(c_idx < nc, c_idx * CMP_STRIDE + (CMP_BLOCK - 1), jnp.iinfo(jnp.int32).max)
        t_col = s0 + (lax.broadcasted_iota(jnp.int32, (1, HPG * tp), 1) & (tp - 1))
        valid = c_end <= t_col
        psums = []
        for g in range(g_):
            st = _dot_nt(kc_ref[g], jnp.concatenate(stack(qn_ref, g), axis=0))
            st = jnp.where(valid, st, NEG)
            e = jnp.exp2(st - jnp.maximum(jnp.max(st, axis=0, keepdims=True), 0.1 * NEG))
            den = jnp.sum(e, axis=0, keepdims=True)
            pt = e * (1.0 / jnp.where(den > 0, den, 1.0))
            o_all = _dot(pt.astype(BF16).T, vc_ref[g])
            for blk in range(PRE_BLOCKS):
                oc_ref[blk, g] = jnp.concatenate(
                    [o_all[hh * tp + blk * tq:hh * tp + (blk + 1) * tq] for hh in range(HPG)], axis=0)
            psum = pt[:, 0:tp]
            for i in range(1, HPG):
                psum = psum + pt[:, i * tp:(i + 1) * tp]
            psums.append(psum)

        psum = jnp.concatenate(psums, axis=1)
        imp = sum(_dot(ov_ref[...], part) for part in _split3(psum))[0:ns, :]
        n_i = lax.broadcasted_iota(jnp.int32, imp.shape, 0)
        tt = s0 + (lax.broadcasted_iota(jnp.int32, imp.shape, 1) & (tp - 1))
        cur = tt // SEL_BLOCK
        sel = (n_i == 0) | (n_i == cur) | (n_i == cur - 1)
        score = jnp.where(sel | (n_i * SEL_BLOCK > tt), -jnp.inf, imp)
        for _ in range(ktop - 3):
            mx = jnp.max(score, axis=0, keepdims=True)
            idx = jnp.min(jnp.where(score == mx, n_i, ns), axis=0, keepdims=True)
            hit = n_i == idx
            sel = sel | hit
            score = jnp.where(hit, -jnp.inf, score)
        bias_t = jnp.where(sel, 0.0, SEL_BIAS).astype(BF16)
        bias_t = jnp.concatenate([bias_t, jnp.zeros((SLOT - ns, g_ * tp), BF16)], axis=0)
        for g in range(g_):
            for blk in range(PRE_BLOCKS):
                c0 = g * tp + blk * tq
                bias_ref[blk, g] = bias_t[:, c0:c0 + tq].T

    pre = qi % PRE_BLOCKS
    qa = []
    for g in range(g_):
        bias = bias_ref[pre, g]
        qa.append(jnp.concatenate([q + bias for q in stack(qr_ref, g)], axis=0))

    nchunk = (qi + 2) // 2
    dif = lax.broadcasted_iota(jnp.int32, (rows, kch), 1) - r_row

    def run_chunks(n, fn):
        def body(q, carry):
            fn(UNROLL * q, UNROLL)
            return carry

        lax.fori_loop(0, n // UNROLL, body, 0)
        for rem in range(1, UNROLL):
            @pl.when(n % UNROLL == rem)
            def _(rem=rem):
                fn((n // UNROLL) * UNROLL, rem)

    def online_chunks(j, nk):
        kbase = pl.multiple_of(j * kch, kch)
        ones_k = jnp.ones((nk * kch, SLOT), BF16)
        for g in range(g_):
            halves = []
            for i in range(nk):
                k0 = pl.multiple_of((j + i) * kch, kch)
                sc = _dot_nt(qa[g], ks_ref[pl.ds(k0, kch), g * SLOT:(g + 1) * SLOT])
                sc = jnp.where(dif <= s0 - k0, sc, NEG)
                halves += [sc[:, :SLOT], sc[:, SLOT:]]
            cm = halves[0]
            for hv in halves[1:]:
                cm = jnp.maximum(cm, hv)
            m_old = mx_ref[g]
            m_new = jnp.maximum(m_old, jnp.max(cm, axis=1, keepdims=True))
            alpha = jnp.exp2(m_old - m_new)
            p = jnp.concatenate([jnp.exp2(hv - m_new) for hv in halves], axis=1).astype(BF16)
            v = jnp.concatenate([vs_ref[pl.ds(kbase, nk * kch), g * SLOT:(g + 1) * SLOT], ones_k], axis=1)
            acc = acc_ref[g]
            acc_ref[g] = jnp.concatenate([acc[:, :SLOT] * alpha, acc[:, SLOT:] * alpha], axis=1) + _dot(p, v)
            mx_ref[g] = m_new

    mx_ref[...] = jnp.full(mx_ref.shape, NEG, F32)
    acc_ref[...] = jnp.zeros(acc_ref.shape, F32)
    run_chunks(nchunk, online_chunks)

    ones_w = jnp.ones((wlen, SLOT), BF16)
    nwt = wlen // tq
    wdif = lax.broadcasted_iota(jnp.int32, (rows, tq), 1) - r_row
    pad_bias = jnp.where(lane == 0, SEL_BIAS, 0.0).astype(BF16)
    o_ws = []
    for g in range(g_):
        qw = jnp.concatenate([q + pad_bias for q in stack(qr_ref, g)], axis=0)
        sw = _dot_nt(qw, kwp_ref[pl.ds(s0, wlen), g * SLOT:(g + 1) * SLOT])
        tiles = [sw[:, i * tq:(i + 1) * tq] for i in range(nwt)]
        tiles[0] = jnp.where(wdif > 0, tiles[0], NEG)
        tiles[-1] = jnp.where(wdif <= 0, tiles[-1], NEG)
        mw = tiles[0]
        for t in tiles[1:]:
            mw = jnp.maximum(mw, t)
        mw = jnp.max(mw, axis=1, keepdims=True)
        pw = jnp.concatenate([jnp.exp2(t - mw) for t in tiles], axis=1).astype(BF16)
        vwin = jnp.concatenate([vwp_ref[pl.ds(s0, wlen), g * SLOT:(g + 1) * SLOT], ones_w], axis=1)
        accw = _dot(pw, vwin)
        o_ws.append(accw[:, :SLOT] / accw[:, SLOT:])

    for g in range(g_):
        acc = acc_ref[g]
        o_s = acc[:, :SLOT] / acc[:, SLOT:]
        o_w = o_ws[g]
        for hp in range(HPG // 2):
            col = (g * HPG + 2 * hp) * D_V
            ev, od = slice(2 * hp * tq, (2 * hp + 1) * tq), slice((2 * hp + 1) * tq, (2 * hp + 2) * tq)
            tile = jnp.zeros((tq, SLOT), F32)
            for br, o_b in enumerate((oc_ref[pre, g], o_s, o_w)):
                pair = jnp.where(lane < D_V, o_b[ev], o_b[od])
                tile = tile + gx_ref[pre, :, br * hw + col:br * hw + col + SLOT] * pair
            o_ref[:, col:col + SLOT] = tile.astype(BF16)


def _nsa_attn(qn, qr, gates, kc, vc, ks, vs, kw, vw, *, nb, s):
    tq = Q_BLOCK
    nq = s // tq
    nc = (s - CMP_BLOCK) // CMP_STRIDE + 1
    ncp = s // CMP_STRIDE
    ns = s // SEL_BLOCK
    ktop = min(SEL_TOPK, ns)
    g_ = N_KV_GROUPS
    assert ncp % 8 == 0 and ns <= AUG and ktop >= 3 and s % (2 * tq) == 0 and 2 * D_V == SLOT
    assert nq % PRE_BLOCKS == 0
    hw = N_HEADS * D_V
    cs = np.arange(ncp) * CMP_STRIDE
    ss = np.arange(SLOT) * SEL_BLOCK
    ov = ((cs[None, :] < ss[:, None] + SEL_BLOCK) & (cs[None, :] + CMP_BLOCK > ss[:, None])
          & (np.arange(ncp)[None, :] < nc) & (np.arange(SLOT)[:, None] < ns))
    ex = np.zeros((LANES, N_BRANCH * hw), np.float32)
    for hd in range(N_HEADS):
        for br in range(N_BRANCH):
            ex[hd * N_BRANCH + br, br * hw + hd * D_V:br * hw + (hd + 1) * D_V] = 1.0
    qspec = lambda wd: pl.BlockSpec((tq, wd), lambda b, q: (b * nq + q, 0))
    pspec = lambda wd: pl.BlockSpec((PRE_BLOCKS * tq, wd), lambda b, q: ((b * nq + q) // PRE_BLOCKS, 0))
    seq = lambda wd: pl.BlockSpec((s, wd), lambda b, q: (b, 0))
    cmp_spec = pl.BlockSpec((None, g_, ncp, SLOT), lambda b, q: (b, 0, 0, 0))
    nkc = s // (2 * tq)
    return pl.pallas_call(
        functools.partial(_nsa_kernel, nc=nc, ns=ns, ktop=ktop),
        grid=(nb, nq),
        in_specs=[pspec(N_HEADS * SLOT), qspec(N_HEADS * SLOT), pspec(LANES), cmp_spec, cmp_spec,
                  seq(g_ * SLOT), seq(g_ * SLOT), seq(g_ * SLOT), seq(g_ * SLOT),
                  _const_spec(ov.shape), _const_spec(ex.shape)],
        out_specs=qspec(hw),
        out_shape=jax.ShapeDtypeStruct((nb * s, hw), BF16),
        scratch_shapes=[pltpu.VMEM((g_, HPG * tq, SLOT), F32),
                        pltpu.VMEM((g_, HPG * tq, 2 * SLOT), F32),
                        pltpu.VMEM((s + WINDOW, g_ * SLOT), BF16),
                        pltpu.VMEM((s + WINDOW, g_ * SLOT), BF16),
                        pltpu.VMEM((PRE_BLOCKS, g_, HPG * tq, SLOT), F32),
                        pltpu.VMEM((PRE_BLOCKS, g_, tq, SLOT), BF16),
                        pltpu.VMEM((PRE_BLOCKS, tq, N_BRANCH * hw), F32)],
        compiler_params=_cparams(("parallel", "arbitrary")), name="nsa_attn",
    )(qn, qr, gates, kc.reshape(nb, g_, ncp, SLOT), vc.reshape(nb, g_, ncp, SLOT), ks, vs, kw, vw,
      jnp.asarray(ov, BF16), jnp.asarray(ex, BF16))


def kernel(x, norm_g, ffn_w_gu, ffn_w_down, a_w_in, a_conv_w, a_conv_b, a_w_ra, a_b_ra, a_w_ix, a_b_ix,
           a_lambda, a_w_out, kv_norm_g, kv_w, cmp_pos_k, cmp_w1_k, cmp_w2_k, cmp_pos_v, cmp_w1_v,
           cmp_w2_v, b_w_q, b_gate_bias, b_w_o, final_g):
    nb, s, d = x.shape
    n_a, n_b = a_w_in.shape[0], b_w_q.shape[0]
    depth = n_a + n_b
    assert n_a >= 1 and norm_g.shape[0] == depth

    h = x
    for layer in range(n_a):
        hm = _rglru_mixer(h, norm_g[layer, 0], a_w_in[layer], a_conv_w[layer], a_conv_b[layer],
                          a_w_ra[layer], a_b_ra[layer], a_w_ix[layer], a_b_ix[layer], a_lambda[layer],
                          a_w_out[layer])
        h = _ffn(hm.reshape(nb * s, d), norm_g[layer, 1], ffn_w_gu[layer], ffn_w_down[layer],
                 final_g=final_g if layer == depth - 1 else None).reshape(nb, s, d)
    if n_b == 0:
        return h
    h = h.reshape(nb * s, d)

    tables = _rope_tables(s)
    xk, xv, ks, vs, kw, vw = _kv_proj(h, kv_norm_g, kv_w, tables, nb=nb, s=s)
    nchunk = s // CMP_STRIDE
    w2k = jnp.pad(cmp_w2_k, ((0, 0), (AUG, 0)))
    w2v = jnp.tile(cmp_w2_v, (1, SLOT // D_V))
    kc = _compress(xk.reshape(-1, xk.shape[-1]), cmp_pos_k, cmp_w1_k, w2k, nchunk=nchunk)
    vc = _compress(xv.reshape(-1, xv.shape[-1]), cmp_pos_v, cmp_w1_v, w2v, nchunk=nchunk)

    for j in range(n_b):
        layer = n_a + j
        qn, qr, gates = _q_proj(h, norm_g[layer, 0], b_w_q[j], b_gate_bias[j], tables, s=s)
        attn = _nsa_attn(qn, qr, gates, kc, vc, ks, vs, kw, vw, nb=nb, s=s)
        h = _ffn(h, norm_g[layer, 1], ffn_w_gu[layer], ffn_w_down[layer], attn=attn, w_o=b_w_o[j],
                 final_g=final_g if layer == depth - 1 else None)
    return h.reshape(nb, s, d)
```

```python
import functools
import math

import numpy as np
import jax
import jax.numpy as jnp
from jax import lax
from jax.experimental import pallas as pl
from jax.experimental.pallas import tpu as pltpu

CONV_WIDTH = 4
RGLRU_C = 8.0
N_HEADS = 16
N_KV_GROUPS = 4
HPG = N_HEADS // N_KV_GROUPS
D_QK = 96
D_V = 64
ROPE_DIMS = D_QK // 4
ROPE_HALF = ROPE_DIMS // 2
ROPE_THETA = 500000.0
CMP_BLOCK = 32
CMP_STRIDE = 16
SEL_BLOCK = 64
SEL_TOPK = 8
WINDOW = 512
Q_BLOCK = 128
N_BRANCH = 3
EPS = 1e-6

LANES = 128
MXU_TILE = 256
SLOT = 128
AUG = SLOT - D_QK
VMEM_LIMIT = 56 * 1024 * 1024
NEG = -1e30
SEL_BIAS = -1e9
LOG2E = math.log2(math.e)
UNROLL = 4
PRE_BLOCKS = 2

F32 = jnp.float32
BF16 = jnp.bfloat16


def _cparams(sem):
    return pltpu.CompilerParams(dimension_semantics=sem, vmem_limit_bytes=VMEM_LIMIT)


def _const_spec(shape):
    nd = len(shape)
    return pl.BlockSpec(shape, lambda *_: (0,) * nd, pipeline_mode=pl.Buffered(1))


def _rms(x, g):
    return x * lax.rsqrt(jnp.mean(x * x, axis=-1, keepdims=True) + EPS) * g


def _gelu(x):
    k1 = -2.0 * math.sqrt(2.0 / math.pi) * LOG2E
    return x / (1.0 + jnp.exp2(x * (k1 + (k1 * 0.044715) * (x * x))))


def _dot(a, b):
    return jnp.dot(a, b, preferred_element_type=F32)


def _dot_nt(a, b):
    return lax.dot_general(a, b, (((1,), (1,)), ((), ())), preferred_element_type=F32)


def _rglru_kernel(x_hbm, g_ref, win_ref, cw_ref, cb_ref, wg_ref, bra_ref, bix_ref, lam_ref, wout_ref,
                  o_hbm, xbe_ref, h_ref, ry_ref, xt_ref, ot_ref, sem_in, sem_out, *, ts, nb, cw, nch, nsteps):
    rows = ts * nb
    halo = (CONV_WIDTH - 1) * nb
    d = xt_ref.shape[-1]
    i = pl.program_id(0)
    slot = i % 2

    def in_copies(step, sl):
        return [pltpu.make_async_copy(x_hbm.at[:, step * ts + t, :], xt_ref.at[sl, t], sem_in.at[sl])
                for t in range(ts)]

    def out_copies(step, sl):
        return [pltpu.make_async_copy(ot_ref.at[sl, t], o_hbm.at[:, step * ts + t, :], sem_out.at[sl])
                for t in range(ts)]

    @pl.when(i == 0)
    def _():
        xbe_ref[:, 0:halo, :] = jnp.zeros((nch, halo, cw), F32)
        h_ref[...] = jnp.zeros_like(h_ref)
        for cp in in_copies(0, 0):
            cp.start()

    @pl.when(i + 1 < nsteps)
    def _():
        for cp in in_copies(i + 1, 1 - slot):
            cp.start()

    for cp in in_copies(i, slot):
        cp.wait()

    hn = _rms(xt_ref[slot].reshape(rows, d), g_ref[...]).astype(BF16)
    for c in range(nch):
        u = _dot(hn, win_ref[c])
        xbe_ref[c, halo:halo + rows, :] = u[:, :cw]
        y = _gelu(u[:, cw:])
        xc = cb_ref[c]
        for j in range(CONV_WIDTH):
            xc = xc + cw_ref[c, j:j + 1, :] * xbe_ref[c, j * nb:j * nb + rows, :]
        xbe_ref[c, 0:halo, :] = xbe_ref[c, rows:rows + halo, :]
        gt = _dot(xc.astype(BF16), wg_ref[c])
        r = jax.nn.sigmoid(gt[:, :cw] + bra_ref[c])
        ig = jax.nn.sigmoid(gt[:, cw:] + bix_ref[c])
        nla = r * (RGLRU_C * jax.nn.softplus(-lam_ref[c]))
        a = jnp.exp2(nla * (-LOG2E))
        om = jnp.tanh(nla) * (1.0 + a * a)
        bx = jnp.where(om > 0, om * lax.rsqrt(om), 0.0) * (ig * xc)
        h = h_ref[c]
        for t in range(ts):
            sl = slice(t * nb, (t + 1) * nb)
            h = a[sl] * h + bx[sl]
            ry_ref[sl, c * cw:(c + 1) * cw] = (h * y[sl]).astype(BF16)
        h_ref[c] = h
    out = _dot(ry_ref[...], wout_ref[...]) + xt_ref[slot].reshape(rows, d)

    @pl.when(i >= 2)
    def _():
        for cp in out_copies(i - 2, slot):
            cp.wait()

    ot_ref[slot] = out.reshape(ts, nb, d)
    for cp in out_copies(i, slot):
        cp.start()

    @pl.when(i == nsteps - 1)
    def _():
        for cp in out_copies(i, slot):
            cp.wait()
        if nsteps > 1:
            for cp in out_copies(i - 1, 1 - slot):
                cp.wait()


def _rglru_mixer(x, g, w_in, conv_w, conv_b, w_ra, b_ra, w_ix, b_ix, lam, w_out):
    nb, s, _ = x.shape
    d = w_in.shape[0]
    r = w_in.shape[1] // 2
    nblk, rb = w_ra.shape[0], w_ra.shape[1]
    per = LANES // math.gcd(rb, LANES)
    cw = per * rb
    nch = nblk // per
    assert nch * per == nblk and nch * cw == r
    ts = 16
    assert s % ts == 0 and nb % 16 == 0
    rows = ts * nb

    def chunks(v):
        return v.reshape(v.shape[0], nch, cw).transpose(1, 0, 2)

    def blockdiag(w):
        w = w.reshape(nch, per, rb, rb)
        return jnp.einsum('cpij,pq->cpiqj', w, jnp.eye(per, dtype=w.dtype)).reshape(nch, cw, cw)

    win = jnp.concatenate([chunks(w_in[:, :r]), chunks(w_in[:, r:])], axis=-1).astype(BF16)
    wg = jnp.concatenate([blockdiag(w_ra), blockdiag(w_ix)], axis=-1).astype(BF16)
    args = (x, g.reshape(1, d), win, chunks(conv_w), chunks(conv_b.reshape(1, r)), wg,
            chunks(b_ra.reshape(1, r)), chunks(b_ix.reshape(1, r)), chunks(lam.reshape(1, r)),
            w_out.astype(BF16))
    hbm = pl.BlockSpec(memory_space=pl.ANY)
    in_specs = [hbm] + [_const_spec(a.shape) for a in args[1:]]
    kern = functools.partial(_rglru_kernel, ts=ts, nb=nb, cw=cw, nch=nch, nsteps=s // ts)
    return pl.pallas_call(
        kern,
        grid=(s // ts,),
        in_specs=in_specs,
        out_specs=hbm,
        out_shape=jax.ShapeDtypeStruct((nb, s, d), F32),
        scratch_shapes=[
            pltpu.VMEM((nch, rows + (CONV_WIDTH - 1) * nb, cw), F32),
            pltpu.VMEM((nch, nb, cw), F32),
            pltpu.VMEM((rows, r), BF16),
            pltpu.VMEM((2, ts, nb, d), F32),
            pltpu.VMEM((2, ts, nb, d), F32),
            pltpu.SemaphoreType.DMA((2,)),
            pltpu.SemaphoreType.DMA((2,)),
        ],
        compiler_params=_cparams(("arbitrary",)),
        name="rglru_mixer",
    )(*args)


def _ffn_kernel(*refs, f, bounds, pre_attn, final_norm):
    it = iter(refs)
    h_ref = next(it)
    if pre_attn:
        attn_ref, wo_ref = next(it), next(it)
    g_ref, wgu_ref, wdn_ref = next(it), next(it), next(it)
    if final_norm:
        gf_ref = next(it)
    o_ref = next(it)

    h = h_ref[...]
    if pre_attn:
        h = h + _dot(attn_ref[...], wo_ref[...])
    hn = _rms(h, g_ref[...]).astype(BF16)
    acc = h
    for lo, hi in zip(bounds[:-1], bounds[1:]):
        gate = _dot(hn, wgu_ref[:, lo:hi])
        val = _dot(hn, wgu_ref[:, f + lo:f + hi])
        act = (jax.nn.silu(gate) * val).astype(BF16)
        acc = acc + _dot(act, wdn_ref[lo:hi, :])
    if final_norm:
        acc = _rms(acc, gf_ref[...])
    o_ref[...] = acc


def _ffn(h, g, w_gu, w_down, *, attn=None, w_o=None, final_g=None):
    n, d = h.shape
    f = w_down.shape[0]
    tr = 512
    mid = -(-f // (2 * MXU_TILE)) * MXU_TILE
    bounds = (0, mid, f) if mid < f else (0, f)
    assert n % tr == 0 and f % LANES == 0
    pre_attn = attn is not None
    final_norm = final_g is not None
    row_spec = lambda w: pl.BlockSpec((tr, w), lambda i: (i, 0))
    args, specs = [h], [row_spec(d)]
    if pre_attn:
        args += [attn, w_o.astype(BF16)]
        specs += [row_spec(attn.shape[1]), _const_spec(w_o.shape)]
    args += [g.reshape(1, d), w_gu.astype(BF16), w_down.astype(BF16)]
    specs += [_const_spec((1, d)), _const_spec(w_gu.shape), _const_spec(w_down.shape)]
    if final_norm:
        args.append(final_g.reshape(1, d))
        specs.append(_const_spec((1, d)))
    kern = functools.partial(_ffn_kernel, f=f, bounds=bounds, pre_attn=pre_attn, final_norm=final_norm)
    return pl.pallas_call(
        kern, grid=(n // tr,), in_specs=specs, out_specs=row_spec(d),
        out_shape=jax.ShapeDtypeStruct((n, d), F32),
        compiler_params=_cparams(("parallel",)), name="ffn",
    )(*args)


def _rope_slot(x, cos, sin_lo, sin_hi):
    return (x * cos + pltpu.roll(x, SLOT - ROPE_HALF, 1) * sin_lo
            + pltpu.roll(x, ROPE_HALF, 1) * sin_hi)


def _kv_proj_kernel(h_ref, g_ref, w_ref, cos_ref, slo_ref, shi_ref,
                    xk_ref, xv_ref, ks_ref, vs_ref, kw_ref, vw_ref, sc_ref, *, tr, s):
    g_ = N_KV_GROUPS
    hn = _rms(h_ref[...], g_ref[...]).astype(BF16)
    u = _dot(hn, w_ref[...])
    slot = lambda i: u[:, i * SLOT:(i + 1) * SLOT]
    cos, slo, shi = cos_ref[...], slo_ref[...], shi_ref[...]
    pos0 = (pl.program_id(0) % (s // tr)) * tr
    pos = pos0 + lax.broadcasted_iota(jnp.int32, (tr, SLOT), 0)
    lane = lax.broadcasted_iota(jnp.int32, (tr, SLOT), 1)
    onehot = (lane == pos // SEL_BLOCK).astype(F32)

    nj = tr // CMP_STRIDE
    for i in range(2 * g_):
        sc_ref[i] = slot(i)
    for x_ref, base in ((xk_ref, 0), (xv_ref, g_)):
        for gi in range(g_):
            for l in range(CMP_STRIDE):
                x_ref[gi, :, l * SLOT:(l + 1) * SLOT] = sc_ref[base + gi, pl.ds(l, nj, stride=CMP_STRIDE), :]

    vbase = 4 * g_
    for k_ref, v_ref, base, aug in ((ks_ref, vs_ref, 2 * g_, True), (kw_ref, vw_ref, 3 * g_, False)):
        for gi in range(g_):
            k = _rope_slot(slot(base + gi), cos, slo, shi)
            if aug:
                k = k + onehot
            k_ref[:, gi * SLOT:(gi + 1) * SLOT] = k.astype(BF16)
        for pair in range(g_ // 2):
            x = slot(vbase + pair)
            y = pltpu.roll(x, D_V, 1)
            v_ref[:, 2 * pair * SLOT:(2 * pair + 1) * SLOT] = jnp.where(lane < D_V, x, y).astype(BF16)
            v_ref[:, (2 * pair + 1) * SLOT:(2 * pair + 2) * SLOT] = jnp.where(lane < D_V, y, x).astype(BF16)
        vbase += g_ // 2


def _q_proj_kernel(h_ref, g_ref, w_ref, gb_ref, cos_ref, slo_ref, shi_ref, qn_ref, qr_ref, gt_ref):
    hn = _rms(h_ref[...], g_ref[...]).astype(BF16)
    u = _dot(hn, w_ref[...])
    cos, slo, shi = cos_ref[...], slo_ref[...], shi_ref[...]
    scale = D_QK ** -0.5 * LOG2E
    for hd in range(N_HEADS):
        q = u[:, hd * SLOT:(hd + 1) * SLOT] * scale
        qn_ref[:, hd * SLOT:(hd + 1) * SLOT] = q.astype(BF16)
        qr_ref[:, hd * SLOT:(hd + 1) * SLOT] = _rope_slot(q, cos, slo, shi).astype(BF16)
    gt_ref[...] = jax.nn.sigmoid(u[:, N_HEADS * SLOT:] + gb_ref[...])


def _pad_slots(w, n, width, left):
    d = w.shape[0]
    w = w.reshape(d, n, width)
    return jnp.pad(w, ((0, 0), (0, 0), (left, SLOT - width - left))).reshape(d, n * SLOT)


def _rope_tables(s):
    inv = ROPE_THETA ** (-jnp.arange(ROPE_HALF, dtype=F32) * 2.0 / ROPE_DIMS)
    ang = jnp.arange(s, dtype=F32)[:, None] * inv[None, :]
    cos, sin = jnp.cos(ang), jnp.sin(ang)
    z = lambda n: jnp.zeros((s, n), F32)
    o = lambda n: jnp.ones((s, n), F32)
    rest = SLOT - AUG - ROPE_DIMS
    cos_t = jnp.concatenate([o(AUG), cos, cos, o(rest)], axis=1)
    sin_lo = jnp.concatenate([z(AUG), -sin, z(ROPE_HALF), z(rest)], axis=1)
    sin_hi = jnp.concatenate([z(AUG), z(ROPE_HALF), sin, z(rest)], axis=1)
    return cos_t, sin_lo, sin_hi


def _kv_proj(h, g, kv_w, tables, *, nb, s):
    n, d = h.shape
    g_, dk, dv = N_KV_GROUPS, D_QK, D_V
    tr = 1024
    assert n % tr == 0 and s % tr == 0 and tr % CMP_STRIDE == 0
    sizes = [g_ * dk, g_ * dv] * 3
    offs = np.concatenate([[0], np.cumsum(sizes)])
    kc, vc, ksl, vsl, kwn, vwn = [kv_w[:, offs[j]:offs[j + 1]] for j in range(6)]
    assert 2 * dv == SLOT and g_ % 2 == 0
    w = jnp.concatenate([_pad_slots(kc, g_, dk, 0), _pad_slots(vc, g_, dv, 0),
                         _pad_slots(ksl, g_, dk, AUG), _pad_slots(kwn, g_, dk, AUG), vsl, vwn],
                        axis=1).astype(BF16)
    row = lambda wd, dt: (pl.BlockSpec((tr, wd), lambda i: (i, 0)), jax.ShapeDtypeStruct((n, wd), dt))
    nj = tr // CMP_STRIDE
    spt = s // tr
    xcmp = (pl.BlockSpec((None, g_, nj, CMP_STRIDE * SLOT), lambda i: (i // spt, 0, i % spt, 0)),
            jax.ShapeDtypeStruct((nb, g_, s // CMP_STRIDE, CMP_STRIDE * SLOT), F32))
    outs = [xcmp, xcmp, row(g_ * SLOT, BF16), row(g_ * SLOT, BF16), row(g_ * SLOT, BF16), row(g_ * SLOT, BF16)]
    tab_spec = pl.BlockSpec((tr, SLOT), lambda i: (i % spt, 0))
    return pl.pallas_call(
        functools.partial(_kv_proj_kernel, tr=tr, s=s),
        grid=(n // tr,),
        in_specs=[pl.BlockSpec((tr, d), lambda i: (i, 0)), _const_spec((1, d)), _const_spec(w.shape),
                  tab_spec, tab_spec, tab_spec],
        out_specs=[o[0] for o in outs], out_shape=[o[1] for o in outs],
        scratch_shapes=[pltpu.VMEM((2 * g_, tr, SLOT), F32)],
        compiler_params=_cparams(("parallel",)), name="kv_proj",
    )(h, g.reshape(1, d), w, *tables)


def _q_proj(h, g, w_q, gate_bias, tables, *, s):
    n, d = h.shape
    tr = 1024
    nq = N_HEADS * D_QK
    ngate = N_HEADS * N_BRANCH
    assert ngate <= LANES
    w = jnp.concatenate([_pad_slots(w_q[:, :nq], N_HEADS, D_QK, AUG),
                         jnp.pad(w_q[:, nq:], ((0, 0), (0, LANES - ngate)))], axis=1).astype(BF16)
    gb = jnp.pad(gate_bias.reshape(1, ngate), ((0, 0), (0, LANES - ngate)))
    tab_spec = pl.BlockSpec((tr, SLOT), lambda i: (i % (s // tr), 0))
    row = lambda wd, dt: (pl.BlockSpec((tr, wd), lambda i: (i, 0)), jax.ShapeDtypeStruct((n, wd), dt))
    outs = [row(N_HEADS * SLOT, BF16), row(N_HEADS * SLOT, BF16), row(LANES, F32)]
    return pl.pallas_call(
        _q_proj_kernel,
        grid=(n // tr,),
        in_specs=[pl.BlockSpec((tr, d), lambda i: (i, 0)), _const_spec((1, d)), _const_spec(w.shape),
                  _const_spec((1, LANES)), tab_spec, tab_spec, tab_spec],
        out_specs=[o[0] for o in outs], out_shape=[o[1] for o in outs],
        compiler_params=_cparams(("parallel",)), name="q_proj",
    )(h, g.reshape(1, d), w, gb, *tables)


def _compress_kernel(x_ref, plo_ref, phi_ref, w1lo_ref, w1hi_ref, w2_ref, o_ref):
    x = x_ref[...]
    lo = _dot((x + plo_ref[...]).astype(BF16), w1lo_ref[...])
    hi = _dot((x + phi_ref[...]).astype(BF16), w1hi_ref[...])
    pre = lo + pltpu.roll(hi, hi.shape[0] - 1, 0)
    o_ref[...] = _dot(_gelu(pre).astype(BF16), w2_ref[...]).astype(BF16)


def _compress(x, pos, w1, w2_slot, *, nchunk):
    n, width = x.shape
    dh = pos.shape[1]
    hid = w1.shape[1]
    assert CMP_BLOCK == 2 * CMP_STRIDE and width == CMP_STRIDE * SLOT
    tr = 4 * nchunk
    assert n % tr == 0

    def slots(v):
        k = v.shape[1]
        v = jnp.pad(v.reshape(2, CMP_STRIDE, dh, k), ((0, 0), (0, 0), (0, SLOT - dh), (0, 0)))
        return v.reshape(2, width, k)

    p = slots(pos.reshape(CMP_BLOCK * dh, 1))[:, :, 0].reshape(2, 1, width)
    w1s = slots(w1).astype(BF16)
    return pl.pallas_call(
        _compress_kernel,
        grid=(n // tr,),
        in_specs=[pl.BlockSpec((tr, width), lambda i: (i, 0)), _const_spec((1, width)), _const_spec((1, width)),
                  _const_spec((width, hid)), _const_spec((width, hid)), _const_spec((hid, SLOT))],
        out_specs=pl.BlockSpec((tr, SLOT), lambda i: (i, 0)),
        out_shape=jax.ShapeDtypeStruct((n, SLOT), BF16),
        compiler_params=_cparams(("parallel",)), name="compress",
    )(x, p[0], p[1], w1s[0], w1s[1], w2_slot.astype(BF16))


def _split3(x):
    hi = x.astype(BF16)
    r1 = x - hi.astype(F32)
    mid = r1.astype(BF16)
    lo = (r1 - mid.astype(F32)).astype(BF16)
    return hi, mid, lo


def _nsa_kernel(qn_ref, qr_ref, gt_ref, kc_ref, vc_ref, ks_ref, vs_ref, kw_ref, vw_ref, ov_ref, ex_ref,
                o_ref, s_ref, mx_ref, acc_ref, kwp_ref, vwp_ref, oc_ref, bias_ref, gx_ref, *, nc, ns, ktop):
    g_ = N_KV_GROUPS
    tq = Q_BLOCK
    rows = HPG * tq
    kch = 2 * tq
    wlen = WINDOW + tq
    qi = pl.program_id(1)
    s0 = pl.multiple_of(qi * tq, tq)
    hw = N_HEADS * D_V
    lane = lax.broadcasted_iota(jnp.int32, (tq, SLOT), 1)

    @pl.when(qi == 0)
    def _():
        pad_lane = lax.broadcasted_iota(jnp.int32, (WINDOW, g_ * SLOT), 1) & (SLOT - 1)
        kwp_ref[0:WINDOW, :] = (pad_lane == 0).astype(BF16)
        vwp_ref[0:WINDOW, :] = jnp.zeros((WINDOW, g_ * SLOT), BF16)
        kwp_ref[WINDOW:, :] = kw_ref[...]
        vwp_ref[WINDOW:, :] = vw_ref[...]

    r_row = lax.broadcasted_iota(jnp.int32, (rows, 1), 0) & (tq - 1)
    stack = lambda ref, g: [ref[:, hd * SLOT:(hd + 1) * SLOT] for hd in range(g * HPG, (g + 1) * HPG)]

    @pl.when(qi % PRE_BLOCKS == 0)
    def _():
        tp = PRE_BLOCKS * tq
        ghi, gmid, _ = _split3(gt_ref[...])
        gx_all = _dot(ghi, ex_ref[...]) + _dot(gmid, ex_ref[...])
        for blk in range(PRE_BLOCKS):
            gx_ref[blk] = gx_all[blk * tq:(blk + 1) * tq]
        ncp = kc_ref.shape[1]
        c_idx = lax.broadcasted_iota(jnp.int32, (ncp, 1), 0)
        c_end = jnp.where(c_idx < nc, c_idx * CMP_STRIDE + (CMP_BLOCK - 1), jnp.iinfo(jnp.int32).max)
        t_col = s0 + (lax.broadcasted_iota(jnp.int32, (1, HPG * tp), 1) & (tp - 1))
        valid = c_end <= t_col
        psums = []
        for g in range(g_):
            st = _dot_nt(kc_ref[g], jnp.concatenate(stack(qn_ref, g), axis=0))
            st = jnp.where(valid, st, NEG)
            e = jnp.exp2(st - jnp.maximum(jnp.max(st, axis=0, keepdims=True), 0.1 * NEG))
            den = jnp.sum(e, axis=0, keepdims=True)
            pt = e * (1.0 / jnp.where(den > 0, den, 1.0))
            o_all = _dot(pt.astype(BF16).T, vc_ref[g])
            for blk in range(PRE_BLOCKS):
                oc_ref[blk, g] = jnp.concatenate(
                    [o_all[hh * tp + blk * tq:hh * tp + (blk + 1) * tq] for hh in range(HPG)], axis=0)
            psum = pt[:, 0:tp]
            for i in range(1, HPG):
                psum = psum + pt[:, i * tp:(i + 1) * tp]
            psums.append(psum)

        psum = jnp.concatenate(psums, axis=1)
        imp = sum(_dot(ov_ref[...], part) for part in _split3(psum))[0:ns, :]
        n_i = lax.broadcasted_iota(jnp.int32, imp.shape, 0)
        tt = s0 + (lax.broadcasted_iota(jnp.int32, imp.shape, 1) & (tp - 1))
        cur = tt // SEL_BLOCK
        sel = (n_i == 0) | (n_i == cur) | (n_i == cur - 1)
        score = jnp.where(sel | (n_i * SEL_BLOCK > tt), -jnp.inf, imp)
        for _ in range(ktop - 3):
            mx = jnp.max(score, axis=0, keepdims=True)
            idx = jnp.min(jnp.where(score == mx, n_i, ns), axis=0, keepdims=True)
            hit = n_i == idx
            sel = sel | hit
            score = jnp.where(hit, -jnp.inf, score)
        bias_t = jnp.where(sel, 0.0, SEL_BIAS).astype(BF16)
        bias_t = jnp.concatenate([bias_t, jnp.zeros((SLOT - ns, g_ * tp), BF16)], axis=0)
        for g in range(g_):
            for blk in range(PRE_BLOCKS):
                c0 = g * tp + blk * tq
                bias_ref[blk, g] = bias_t[:, c0:c0 + tq].T

    pre = qi % PRE_BLOCKS
    qa = []
    for g in range(g_):
        bias = bias_ref[pre, g]
        qa.append(jnp.concatenate([q + bias for q in stack(qr_ref, g)], axis=0))

    nchunk = (qi + 2) // 2
    dif = lax.broadcasted_iota(jnp.int32, (rows, kch), 1) - r_row

    def run_chunks(n, fn):
        def body(q, carry):
            fn(UNROLL * q, UNROLL)
            return carry

        lax.fori_loop(0, n // UNROLL, body, 0)
        for rem in range(1, UNROLL):
            @pl.when(n % UNROLL == rem)
            def _(rem=rem):
                fn((n // UNROLL) * UNROLL, rem)

    def score_chunks(j, nk):
        for g in range(g_):
            mg = mx_ref[g]
            for i in range(nk):
                k0 = pl.multiple_of((j + i) * kch, kch)
                sc = _dot_nt(qa[g], ks_ref[pl.ds(k0, kch), g * SLOT:(g + 1) * SLOT])
                sc = jnp.where(dif <= s0 - k0, sc, NEG)
                lo, hi = sc[:, :SLOT], sc[:, SLOT:]
                s_ref[g, j + i, 0] = lo
                s_ref[g, j + i, 1] = hi
                mg = jnp.maximum(mg, jnp.maximum(lo, hi))
            mx_ref[g] = mg

    mx_ref[...] = jnp.full(mx_ref.shape, NEG, F32)
    run_chunks(nchunk, score_chunks)
    for g in range(g_):
        mx_ref[g] = jnp.broadcast_to(jnp.max(mx_ref[g], axis=1, keepdims=True), (rows, SLOT))
    acc_ref[...] = jnp.zeros(acc_ref.shape, F32)

    def pv_chunks(j, nk):
        k0 = pl.multiple_of(j * kch, kch)
        ones_k = jnp.ones((nk * kch, SLOT), BF16)
        for g in range(g_):
            m = mx_ref[g]
            parts = []
            for i in range(nk):
                parts += [jnp.exp2(s_ref[g, j + i, 0] - m), jnp.exp2(s_ref[g, j + i, 1] - m)]
            p = jnp.concatenate(parts, axis=1).astype(BF16)
            v = jnp.concatenate([vs_ref[pl.ds(k0, nk * kch), g * SLOT:(g + 1) * SLOT], ones_k], axis=1)
            acc_ref[g] += _dot(p, v)

    run_chunks(nchunk, pv_chunks)

    ones_w = jnp.ones((wlen, SLOT), BF16)
    nwt = wlen // tq
    wdif = lax.broadcasted_iota(jnp.int32, (rows, tq), 1) - r_row
    pad_bias = jnp.where(lane == 0, SEL_BIAS, 0.0).astype(BF16)
    o_ws = []
    for g in range(g_):
        qw = jnp.concatenate([q + pad_bias for q in stack(qr_ref, g)], axis=0)
        sw = _dot_nt(qw, kwp_ref[pl.ds(s0, wlen), g * SLOT:(g + 1) * SLOT])
        tiles = [sw[:, i * tq:(i + 1) * tq] for i in range(nwt)]
        tiles[0] = jnp.where(wdif > 0, tiles[0], NEG)
        tiles[-1] = jnp.where(wdif <= 0, tiles[-1], NEG)
        mw = tiles[0]
        for t in tiles[1:]:
            mw = jnp.maximum(mw, t)
        mw = jnp.max(mw, axis=1, keepdims=True)
        pw = jnp.concatenate([jnp.exp2(t - mw) for t in tiles], axis=1).astype(BF16)
        vwin = jnp.concatenate([vwp_ref[pl.ds(s0, wlen), g * SLOT:(g + 1) * SLOT], ones_w], axis=1)
        accw = _dot(pw, vwin)
        o_ws.append(accw[:, :SLOT] / accw[:, SLOT:])

    for g in range(g_):
        acc = acc_ref[g]
        o_s = acc[:, :SLOT] / acc[:, SLOT:]
        o_w = o_ws[g]
        for hp in range(HPG // 2):
            col = (g * HPG + 2 * hp) * D_V
            ev, od = slice(2 * hp * tq, (2 * hp + 1) * tq), slice((2 * hp + 1) * tq, (2 * hp + 2) * tq)
            tile = jnp.zeros((tq, SLOT), F32)
            for br, o_b in enumerate((oc_ref[pre, g], o_s, o_w)):
                pair = jnp.where(lane < D_V, o_b[ev], o_b[od])
                tile = tile + gx_ref[pre, :, br * hw + col:br * hw + col + SLOT] * pair
            o_ref[:, col:col + SLOT] = tile.astype(BF16)


def _nsa_attn(qn, qr, gates, kc, vc, ks, vs, kw, vw, *, nb, s):
    tq = Q_BLOCK
    nq = s // tq
    nc = (s - CMP_BLOCK) // CMP_STRIDE + 1
    ncp = s // CMP_STRIDE
    ns = s // SEL_BLOCK
    ktop = min(SEL_TOPK, ns)
    g_ = N_KV_GROUPS
    assert ncp % 8 == 0 and ns <= AUG and ktop >= 3 and s % (2 * tq) == 0 and 2 * D_V == SLOT
    assert nq % PRE_BLOCKS == 0
    hw = N_HEADS * D_V
    cs = np.arange(ncp) * CMP_STRIDE
    ss = np.arange(SLOT) * SEL_BLOCK
    ov = ((cs[None, :] < ss[:, None] + SEL_BLOCK) & (cs[None, :] + CMP_BLOCK > ss[:, None])
          & (np.arange(ncp)[None, :] < nc) & (np.arange(SLOT)[:, None] < ns))
    ex = np.zeros((LANES, N_BRANCH * hw), np.float32)
    for hd in range(N_HEADS):
        for br in range(N_BRANCH):
            ex[hd * N_BRANCH + br, br * hw + hd * D_V:br * hw + (hd + 1) * D_V] = 1.0
    qspec = lambda wd: pl.BlockSpec((tq, wd), lambda b, q: (b * nq + q, 0))
    pspec = lambda wd: pl.BlockSpec((PRE_BLOCKS * tq, wd), lambda b, q: ((b * nq + q) // PRE_BLOCKS, 0))
    seq = lambda wd: pl.BlockSpec((s, wd), lambda b, q: (b, 0))
    cmp_spec = pl.BlockSpec((None, g_, ncp, SLOT), lambda b, q: (b, 0, 0, 0))
    nkc = s // (2 * tq)
    return pl.pallas_call(
        functools.partial(_nsa_kernel, nc=nc, ns=ns, ktop=ktop),
        grid=(nb, nq),
        in_specs=[pspec(N_HEADS * SLOT), qspec(N_HEADS * SLOT), pspec(LANES), cmp_spec, cmp_spec,
                  seq(g_ * SLOT), seq(g_ * SLOT), seq(g_ * SLOT), seq(g_ * SLOT),
                  _const_spec(ov.shape), _const_spec(ex.shape)],
        out_specs=qspec(hw),
        out_shape=jax.ShapeDtypeStruct((nb * s, hw), BF16),
        scratch_shapes=[pltpu.VMEM((g_, nkc, 2 * tq // SLOT, HPG * tq, SLOT), F32),
                        pltpu.VMEM((g_, HPG * tq, SLOT), F32),
                        pltpu.VMEM((g_, HPG * tq, 2 * SLOT), F32),
                        pltpu.VMEM((s + WINDOW, g_ * SLOT), BF16),
                        pltpu.VMEM((s + WINDOW, g_ * SLOT), BF16),
                        pltpu.VMEM((PRE_BLOCKS, g_, HPG * tq, SLOT), F32),
                        pltpu.VMEM((PRE_BLOCKS, g_, tq, SLOT), BF16),
                        pltpu.VMEM((PRE_BLOCKS, tq, N_BRANCH * hw), F32)],
        compiler_params=_cparams(("parallel", "arbitrary")), name="nsa_attn",
    )(qn, qr, gates, kc.reshape(nb, g_, ncp, SLOT), vc.reshape(nb, g_, ncp, SLOT), ks, vs, kw, vw,
      jnp.asarray(ov, BF16), jnp.asarray(ex, BF16))


def kernel(x, norm_g, ffn_w_gu, ffn_w_down, a_w_in, a_conv_w, a_conv_b, a_w_ra, a_b_ra, a_w_ix, a_b_ix,
           a_lambda, a_w_out, kv_norm_g, kv_w, cmp_pos_k, cmp_w1_k, cmp_w2_k, cmp_pos_v, cmp_w1_v,
           cmp_w2_v, b_w_q, b_gate_bias, b_w_o, final_g):
    nb, s, d = x.shape
    n_a, n_b = a_w_in.shape[0], b_w_q.shape[0]
    depth = n_a + n_b
    assert n_a >= 1 and norm_g.shape[0] == depth

    h = x
    for layer in range(n_a):
        hm = _rglru_mixer(h, norm_g[layer, 0], a_w_in[layer], a_conv_w[layer], a_conv_b[layer],
                          a_w_ra[layer], a_b_ra[layer], a_w_ix[layer], a_b_ix[layer], a_lambda[layer],
                          a_w_out[layer])
        h = _ffn(hm.reshape(nb * s, d), norm_g[layer, 1], ffn_w_gu[layer], ffn_w_down[layer],
                 final_g=final_g if layer == depth - 1 else None).reshape(nb, s, d)
    if n_b == 0:
        return h
    h = h.reshape(nb * s, d)

    tables = _rope_tables(s)
    xk, xv, ks, vs, kw, vw = _kv_proj(h, kv_norm_g, kv_w, tables, nb=nb, s=s)
    nchunk = s // CMP_STRIDE
    w2k = jnp.pad(cmp_w2_k, ((0, 0), (AUG, 0)))
    w2v = jnp.tile(cmp_w2_v, (1, SLOT // D_V))
    kc = _compress(xk.reshape(-1, xk.shape[-1]), cmp_pos_k, cmp_w1_k, w2k, nchunk=nchunk)
    vc = _compress(xv.reshape(-1, xv.shape[-1]), cmp_pos_v, cmp_w1_v, w2v, nchunk=nchunk)

    for j in range(n_b):
        layer = n_a + j
        qn, qr, gates = _q_proj(h, norm_g[layer, 0], b_w_q[j], b_gate_bias[j], tables, s=s)
        attn = _nsa_attn(qn, qr, gates, kc, vc, ks, vs, kw, vw, nb=nb, s=s)
        h = _ffn(h, norm_g[layer, 1], ffn_w_gu[layer], ffn_w_down[layer], attn=attn, w_o=b_w_o[j],
                 final_g=final_g if layer == depth - 1 else None)
    return h.reshape(nb, s, d)
```

```python
import functools
import math

import numpy as np
import jax
import jax.numpy as jnp
from jax import lax
from jax.experimental import pallas as pl
from jax.experimental.pallas import tpu as pltpu

CONV_WIDTH = 4
RGLRU_C = 8.0
N_HEADS = 16
N_KV_GROUPS = 4
HPG = N_HEADS // N_KV_GROUPS
D_QK = 96
D_V = 64
ROPE_DIMS = D_QK // 4
ROPE_HALF = ROPE_DIMS // 2
ROPE_THETA = 500000.0
CMP_BLOCK = 32
CMP_STRIDE = 16
SEL_BLOCK = 64
SEL_TOPK = 8
WINDOW = 512
Q_BLOCK = 128
N_BRANCH = 3
EPS = 1e-6

LANES = 128
MXU_TILE = 256
SLOT = 128
AUG = SLOT - D_QK
VMEM_LIMIT = 56 * 1024 * 1024
NEG = -1e30
SEL_BIAS = -1e9
LOG2E = math.log2(math.e)
UNROLL = 4
PRE_BLOCKS = 2

F32 = jnp.float32
BF16 = jnp.bfloat16


def _cparams(sem):
    return pltpu.CompilerParams(dimension_semantics=sem, vmem_limit_bytes=VMEM_LIMIT)


def _const_spec(shape):
    nd = len(shape)
    return pl.BlockSpec(shape, lambda *_: (0,) * nd, pipeline_mode=pl.Buffered(1))


def _rms(x, g):
    return x * lax.rsqrt(jnp.mean(x * x, axis=-1, keepdims=True) + EPS) * g


def _gelu(x):
    k1 = -2.0 * math.sqrt(2.0 / math.pi) * LOG2E
    return x / (1.0 + jnp.exp2(x * (k1 + (k1 * 0.044715) * (x * x))))


def _dot(a, b):
    return jnp.dot(a, b, preferred_element_type=F32)


def _dot_nt(a, b):
    return lax.dot_general(a, b, (((1,), (1,)), ((), ())), preferred_element_type=F32)


def _rglru_kernel(x_hbm, g_ref, win_ref, cw_ref, cb_ref, wg_ref, bra_ref, bix_ref, lam_ref, wout_ref,
                  o_hbm, xbe_ref, h_ref, ry_ref, xt_ref, ot_ref, sem_in, sem_out, *, ts, nb, cw, nch, nsteps):
    rows = ts * nb
    halo = (CONV_WIDTH - 1) * nb
    d = xt_ref.shape[-1]
    i = pl.program_id(0)
    slot = i % 2

    def in_copies(step, sl):
        return [pltpu.make_async_copy(x_hbm.at[:, step * ts + t, :], xt_ref.at[sl, t], sem_in.at[sl])
                for t in range(ts)]

    def out_copies(step, sl):
        return [pltpu.make_async_copy(ot_ref.at[sl, t], o_hbm.at[:, step * ts + t, :], sem_out.at[sl])
                for t in range(ts)]

    @pl.when(i == 0)
    def _():
        xbe_ref[:, 0:halo, :] = jnp.zeros((nch, halo, cw), F32)
        h_ref[...] = jnp.zeros_like(h_ref)
        for cp in in_copies(0, 0):
            cp.start()

    @pl.when(i + 1 < nsteps)
    def _():
        for cp in in_copies(i + 1, 1 - slot):
            cp.start()

    for cp in in_copies(i, slot):
        cp.wait()

    hn = _rms(xt_ref[slot].reshape(rows, d), g_ref[...]).astype(BF16)
    for c in range(nch):
        u = _dot(hn, win_ref[c])
        xbe_ref[c, halo:halo + rows, :] = u[:, :cw]
        y = _gelu(u[:, cw:])
        xc = cb_ref[c]
        for j in range(CONV_WIDTH):
            xc = xc + cw_ref[c, j:j + 1, :] * xbe_ref[c, j * nb:j * nb + rows, :]
        xbe_ref[c, 0:halo, :] = xbe_ref[c, rows:rows + halo, :]
        gt = _dot(xc.astype(BF16), wg_ref[c])
        r = jax.nn.sigmoid(gt[:, :cw] + bra_ref[c])
        ig = jax.nn.sigmoid(gt[:, cw:] + bix_ref[c])
        nla = r * (RGLRU_C * jax.nn.softplus(-lam_ref[c]))
        a = jnp.exp2(nla * (-LOG2E))
        om = jnp.tanh(nla) * (1.0 + a * a)
        bx = jnp.where(om > 0, om * lax.rsqrt(om), 0.0) * (ig * xc)
        h = h_ref[c]
        for t in range(ts):
            sl = slice(t * nb, (t + 1) * nb)
            h = a[sl] * h + bx[sl]
            ry_ref[sl, c * cw:(c + 1) * cw] = (h * y[sl]).astype(BF16)
        h_ref[c] = h
    out = _dot(ry_ref[...], wout_ref[...]) + xt_ref[slot].reshape(rows, d)

    @pl.when(i >= 2)
    def _():
        for cp in out_copies(i - 2, slot):
            cp.wait()

    ot_ref[slot] = out.reshape(ts, nb, d)
    for cp in out_copies(i, slot):
        cp.start()

    @pl.when(i == nsteps - 1)
    def _():
        for cp in out_copies(i, slot):
            cp.wait()
        if nsteps > 1:
            for cp in out_copies(i - 1, 1 - slot):
                cp.wait()


def _rglru_mixer(x, g, w_in, conv_w, conv_b, w_ra, b_ra, w_ix, b_ix, lam, w_out):
    nb, s, _ = x.shape
    d = w_in.shape[0]
    r = w_in.shape[1] // 2
    nblk, rb = w_ra.shape[0], w_ra.shape[1]
    per = LANES // math.gcd(rb, LANES)
    cw = per * rb
    nch = nblk // per
    assert nch * per == nblk and nch * cw == r
    ts = 16
    assert s % ts == 0 and nb % 16 == 0
    rows = ts * nb

    def chunks(v):
        return v.reshape(v.shape[0], nch, cw).transpose(1, 0, 2)

    def blockdiag(w):
        w = w.reshape(nch, per, rb, rb)
        return jnp.einsum('cpij,pq->cpiqj', w, jnp.eye(per, dtype=w.dtype)).reshape(nch, cw, cw)

    win = jnp.concatenate([chunks(w_in[:, :r]), chunks(w_in[:, r:])], axis=-1).astype(BF16)
    wg = jnp.concatenate([blockdiag(w_ra), blockdiag(w_ix)], axis=-1).astype(BF16)
    args = (x, g.reshape(1, d), win, chunks(conv_w), chunks(conv_b.reshape(1, r)), wg,
            chunks(b_ra.reshape(1, r)), chunks(b_ix.reshape(1, r)), chunks(lam.reshape(1, r)),
            w_out.astype(BF16))
    hbm = pl.BlockSpec(memory_space=pl.ANY)
    in_specs = [hbm] + [_const_spec(a.shape) for a in args[1:]]
    kern = functools.partial(_rglru_kernel, ts=ts, nb=nb, cw=cw, nch=nch, nsteps=s // ts)
    return pl.pallas_call(
        kern,
        grid=(s // ts,),
        in_specs=in_specs,
        out_specs=hbm,
        out_shape=jax.ShapeDtypeStruct((nb, s, d), F32),
        scratch_shapes=[
            pltpu.VMEM((nch, rows + (CONV_WIDTH - 1) * nb, cw), F32),
            pltpu.VMEM((nch, nb, cw), F32),
            pltpu.VMEM((rows, r), BF16),
            pltpu.VMEM((2, ts, nb, d), F32),
            pltpu.VMEM((2, ts, nb, d), F32),
            pltpu.SemaphoreType.DMA((2,)),
            pltpu.SemaphoreType.DMA((2,)),
        ],
        compiler_params=_cparams(("arbitrary",)),
        name="rglru_mixer",
    )(*args)


def _ffn_kernel(*refs, f, bounds, pre_attn, final_norm):
    it = iter(refs)
    h_ref = next(it)
    if pre_attn:
        attn_ref, wo_ref = next(it), next(it)
    g_ref, wgu_ref, wdn_ref = next(it), next(it), next(it)
    if final_norm:
        gf_ref = next(it)
    o_ref = next(it)

    h = h_ref[...]
    if pre_attn:
        h = h + _dot(attn_ref[...], wo_ref[...])
    hn = _rms(h, g_ref[...]).astype(BF16)
    acc = h
    for lo, hi in zip(bounds[:-1], bounds[1:]):
        gate = _dot(hn, wgu_ref[:, lo:hi])
        val = _dot(hn, wgu_ref[:, f + lo:f + hi])
        act = (jax.nn.silu(gate) * val).astype(BF16)
        acc = acc + _dot(act, wdn_ref[lo:hi, :])
    if final_norm:
        acc = _rms(acc, gf_ref[...])
    o_ref[...] = acc


def _ffn(h, g, w_gu, w_down, *, attn=None, w_o=None, final_g=None):
    n, d = h.shape
    f = w_down.shape[0]
    tr = 1024
    step = 3 * MXU_TILE
    bounds = tuple(range(0, f, step)) + (f,)
    assert n % tr == 0 and f % LANES == 0
    pre_attn = attn is not None
    final_norm = final_g is not None
    row_spec = lambda w: pl.BlockSpec((tr, w), lambda i: (i, 0))
    args, specs = [h], [row_spec(d)]
    if pre_attn:
        args += [attn, w_o.astype(BF16)]
        specs += [row_spec(attn.shape[1]), _const_spec(w_o.shape)]
    args += [g.reshape(1, d), w_gu.astype(BF16), w_down.astype(BF16)]
    specs += [_const_spec((1, d)), _const_spec(w_gu.shape), _const_spec(w_down.shape)]
    if final_norm:
        args.append(final_g.reshape(1, d))
        specs.append(_const_spec((1, d)))
    kern = functools.partial(_ffn_kernel, f=f, bounds=bounds, pre_attn=pre_attn, final_norm=final_norm)
    return pl.pallas_call(
        kern, grid=(n // tr,), in_specs=specs, out_specs=row_spec(d),
        out_shape=jax.ShapeDtypeStruct((n, d), F32),
        compiler_params=_cparams(("parallel",)), name="ffn",
    )(*args)


def _rope_slot(x, cos, sin_lo, sin_hi):
    return (x * cos + pltpu.roll(x, SLOT - ROPE_HALF, 1) * sin_lo
            + pltpu.roll(x, ROPE_HALF, 1) * sin_hi)


def _kv_proj_kernel(h_ref, g_ref, w_ref, cos_ref, slo_ref, shi_ref,
                    xk_ref, xv_ref, ks_ref, vs_ref, kw_ref, vw_ref, sc_ref, *, tr, s):
    g_ = N_KV_GROUPS
    hn = _rms(h_ref[...], g_ref[...]).astype(BF16)
    u = _dot(hn, w_ref[...])
    slot = lambda i: u[:, i * SLOT:(i + 1) * SLOT]
    cos, slo, shi = cos_ref[...], slo_ref[...], shi_ref[...]
    pos0 = (pl.program_id(0) % (s // tr)) * tr
    pos = pos0 + lax.broadcasted_iota(jnp.int32, (tr, SLOT), 0)
    lane = lax.broadcasted_iota(jnp.int32, (tr, SLOT), 1)
    onehot = (lane == pos // SEL_BLOCK).astype(F32)

    nj = tr // CMP_STRIDE
    for i in range(2 * g_):
        sc_ref[i] = slot(i)
    for x_ref, base in ((xk_ref, 0), (xv_ref, g_)):
        for gi in range(g_):
            for l in range(CMP_STRIDE):
                x_ref[gi, :, l * SLOT:(l + 1) * SLOT] = sc_ref[base + gi, pl.ds(l, nj, stride=CMP_STRIDE), :]

    vbase = 4 * g_
    for k_ref, v_ref, base, aug in ((ks_ref, vs_ref, 2 * g_, True), (kw_ref, vw_ref, 3 * g_, False)):
        for gi in range(g_):
            k = _rope_slot(slot(base + gi), cos, slo, shi)
            if aug:
                k = k + onehot
            k_ref[:, gi * SLOT:(gi + 1) * SLOT] = k.astype(BF16)
        for pair in range(g_ // 2):
            x = slot(vbase + pair)
            y = pltpu.roll(x, D_V, 1)
            v_ref[:, 2 * pair * SLOT:(2 * pair + 1) * SLOT] = jnp.where(lane < D_V, x, y).astype(BF16)
            v_ref[:, (2 * pair + 1) * SLOT:(2 * pair + 2) * SLOT] = jnp.where(lane < D_V, y, x).astype(BF16)
        vbase += g_ // 2


def _q_proj_kernel(h_ref, g_ref, w_ref, gb_ref, cos_ref, slo_ref, shi_ref, qn_ref, qr_ref, gt_ref):
    hn = _rms(h_ref[...], g_ref[...]).astype(BF16)
    u = _dot(hn, w_ref[...])
    cos, slo, shi = cos_ref[...], slo_ref[...], shi_ref[...]
    scale = D_QK ** -0.5 * LOG2E
    for hd in range(N_HEADS):
        q = u[:, hd * SLOT:(hd + 1) * SLOT] * scale
        qn_ref[:, hd * SLOT:(hd + 1) * SLOT] = q.astype(BF16)
        qr_ref[:, hd * SLOT:(hd + 1) * SLOT] = _rope_slot(q, cos, slo, shi).astype(BF16)
    gt_ref[...] = jax.nn.sigmoid(u[:, N_HEADS * SLOT:] + gb_ref[...])


def _pad_slots(w, n, width, left):
    d = w.shape[0]
    w = w.reshape(d, n, width)
    return jnp.pad(w, ((0, 0), (0, 0), (left, SLOT - width - left))).reshape(d, n * SLOT)


def _rope_tables(s):
    inv = ROPE_THETA ** (-jnp.arange(ROPE_HALF, dtype=F32) * 2.0 / ROPE_DIMS)
    ang = jnp.arange(s, dtype=F32)[:, None] * inv[None, :]
    cos, sin = jnp.cos(ang), jnp.sin(ang)
    z = lambda n: jnp.zeros((s, n), F32)
    o = lambda n: jnp.ones((s, n), F32)
    rest = SLOT - AUG - ROPE_DIMS
    cos_t = jnp.concatenate([o(AUG), cos, cos, o(rest)], axis=1)
    sin_lo = jnp.concatenate([z(AUG), -sin, z(ROPE_HALF), z(rest)], axis=1)
    sin_hi = jnp.concatenate([z(AUG), z(ROPE_HALF), sin, z(rest)], axis=1)
    return cos_t, sin_lo, sin_hi


def _kv_proj(h, g, kv_w, tables, *, nb, s):
    n, d = h.shape
    g_, dk, dv = N_KV_GROUPS, D_QK, D_V
    tr = 1024
    assert n % tr == 0 and s % tr == 0 and tr % CMP_STRIDE == 0
    sizes = [g_ * dk, g_ * dv] * 3
    offs = np.concatenate([[0], np.cumsum(sizes)])
    kc, vc, ksl, vsl, kwn, vwn = [kv_w[:, offs[j]:offs[j + 1]] for j in range(6)]
    assert 2 * dv == SLOT and g_ % 2 == 0
    w = jnp.concatenate([_pad_slots(kc, g_, dk, 0), _pad_slots(vc, g_, dv, 0),
                         _pad_slots(ksl, g_, dk, AUG), _pad_slots(kwn, g_, dk, AUG), vsl, vwn],
                        axis=1).astype(BF16)
    row = lambda wd, dt: (pl.BlockSpec((tr, wd), lambda i: (i, 0)), jax.ShapeDtypeStruct((n, wd), dt))
    nj = tr // CMP_STRIDE
    spt = s // tr
    xcmp = (pl.BlockSpec((None, g_, nj, CMP_STRIDE * SLOT), lambda i: (i // spt, 0, i % spt, 0)),
            jax.ShapeDtypeStruct((nb, g_, s // CMP_STRIDE, CMP_STRIDE * SLOT), F32))
    outs = [xcmp, xcmp, row(g_ * SLOT, BF16), row(g_ * SLOT, BF16), row(g_ * SLOT, BF16), row(g_ * SLOT, BF16)]
    tab_spec = pl.BlockSpec((tr, SLOT), lambda i: (i % spt, 0))
    return pl.pallas_call(
        functools.partial(_kv_proj_kernel, tr=tr, s=s),
        grid=(n // tr,),
        in_specs=[pl.BlockSpec((tr, d), lambda i: (i, 0)), _const_spec((1, d)), _const_spec(w.shape),
                  tab_spec, tab_spec, tab_spec],
        out_specs=[o[0] for o in outs], out_shape=[o[1] for o in outs],
        scratch_shapes=[pltpu.VMEM((2 * g_, tr, SLOT), F32)],
        compiler_params=_cparams(("parallel",)), name="kv_proj",
    )(h, g.reshape(1, d), w, *tables)


def _q_proj(h, g, w_q, gate_bias, tables, *, s):
    n, d = h.shape
    tr = 1024
    nq = N_HEADS * D_QK
    ngate = N_HEADS * N_BRANCH
    assert ngate <= LANES
    w = jnp.concatenate([_pad_slots(w_q[:, :nq], N_HEADS, D_QK, AUG),
                         jnp.pad(w_q[:, nq:], ((0, 0), (0, LANES - ngate)))], axis=1).astype(BF16)
    gb = jnp.pad(gate_bias.reshape(1, ngate), ((0, 0), (0, LANES - ngate)))
    tab_spec = pl.BlockSpec((tr, SLOT), lambda i: (i % (s // tr), 0))
    row = lambda wd, dt: (pl.BlockSpec((tr, wd), lambda i: (i, 0)), jax.ShapeDtypeStruct((n, wd), dt))
    outs = [row(N_HEADS * SLOT, BF16), row(N_HEADS * SLOT, BF16), row(LANES, F32)]
    return pl.pallas_call(
        _q_proj_kernel,
        grid=(n // tr,),
        in_specs=[pl.BlockSpec((tr, d), lambda i: (i, 0)), _const_spec((1, d)), _const_spec(w.shape),
                  _const_spec((1, LANES)), tab_spec, tab_spec, tab_spec],
        out_specs=[o[0] for o in outs], out_shape=[o[1] for o in outs],
        compiler_params=_cparams(("parallel",)), name="q_proj",
    )(h, g.reshape(1, d), w, gb, *tables)


def _compress_kernel(x_ref, plo_ref, phi_ref, w1lo_ref, w1hi_ref, w2_ref, o_ref):
    x = x_ref[...]
    lo = _dot((x + plo_ref[...]).astype(BF16), w1lo_ref[...])
    hi = _dot((x + phi_ref[...]).astype(BF16), w1hi_ref[...])
    pre = lo + pltpu.roll(hi, hi.shape[0] - 1, 0)
    o_ref[...] = _dot(_gelu(pre).astype(BF16), w2_ref[...]).astype(BF16)


def _compress(x, pos, w1, w2_slot, *, nchunk):
    n, width = x.shape
    dh = pos.shape[1]
    hid = w1.shape[1]
    assert CMP_BLOCK == 2 * CMP_STRIDE and width == CMP_STRIDE * SLOT
    tr = 4 * nchunk
    assert n % tr == 0

    def slots(v):
        k = v.shape[1]
        v = jnp.pad(v.reshape(2, CMP_STRIDE, dh, k), ((0, 0), (0, 0), (0, SLOT - dh), (0, 0)))
        return v.reshape(2, width, k)

    p = slots(pos.reshape(CMP_BLOCK * dh, 1))[:, :, 0].reshape(2, 1, width)
    w1s = slots(w1).astype(BF16)
    return pl.pallas_call(
        _compress_kernel,
        grid=(n // tr,),
        in_specs=[pl.BlockSpec((tr, width), lambda i: (i, 0)), _const_spec((1, width)), _const_spec((1, width)),
                  _const_spec((width, hid)), _const_spec((width, hid)), _const_spec((hid, SLOT))],
        out_specs=pl.BlockSpec((tr, SLOT), lambda i: (i, 0)),
        out_shape=jax.ShapeDtypeStruct((n, SLOT), BF16),
        compiler_params=_cparams(("parallel",)), name="compress",
    )(x, p[0], p[1], w1s[0], w1s[1], w2_slot.astype(BF16))


def _split3(x):
    hi = x.astype(BF16)
    r1 = x - hi.astype(F32)
    mid = r1.astype(BF16)
    lo = (r1 - mid.astype(F32)).astype(BF16)
    return hi, mid, lo


def _nsa_kernel(qn_ref, qr_ref, gt_ref, kc_ref, vc_ref, ks_ref, vs_ref, kw_ref, vw_ref, ov_ref, ex_ref,
                o_ref, s_ref, mx_ref, acc_ref, kwp_ref, vwp_ref, oc_ref, bias_ref, gx_ref, *, nc, ns, ktop):
    g_ = N_KV_GROUPS
    tq = Q_BLOCK
    rows = HPG * tq
    kch = 2 * tq
    wlen = WINDOW + tq
    qi = pl.program_id(1)
    s0 = pl.multiple_of(qi * tq, tq)
    hw = N_HEADS * D_V
    lane = lax.broadcasted_iota(jnp.int32, (tq, SLOT), 1)

    @pl.when(qi == 0)
    def _():
        pad_lane = lax.broadcasted_iota(jnp.int32, (WINDOW, g_ * SLOT), 1) & (SLOT - 1)
        kwp_ref[0:WINDOW, :] = (pad_lane == 0).astype(BF16)
        vwp_ref[0:WINDOW, :] = jnp.zeros((WINDOW, g_ * SLOT), BF16)
        kwp_ref[WINDOW:, :] = kw_ref[...]
        vwp_ref[WINDOW:, :] = vw_ref[...]

    r_row = lax.broadcasted_iota(jnp.int32, (rows, 1), 0) & (tq - 1)
    stack = lambda ref, g: [ref[:, hd * SLOT:(hd + 1) * SLOT] for hd in range(g * HPG, (g + 1) * HPG)]

    @pl.when(qi % PRE_BLOCKS == 0)
    def _():
        tp = PRE_BLOCKS * tq
        ghi, gmid, _ = _split3(gt_ref[...])
        gx_all = _dot(ghi, ex_ref[...]) + _dot(gmid, ex_ref[...])
        for blk in range(PRE_BLOCKS):
            gx_ref[blk] = gx_all[blk * tq:(blk + 1) * tq]
        ncp = kc_ref.shape[1]
        c_idx = lax.broadcasted_iota(jnp.int32, (ncp, 1), 0)
        c_end = jnp.where(c_idx < nc, c_idx * CMP_STRIDE + (CMP_BLOCK - 1), jnp.iinfo(jnp.int32).max)
        t_col = s0 + (lax.broadcasted_iota(jnp.int32, (1, HPG * tp), 1) & (tp - 1))
        valid = c_end <= t_col
        psums = []
        for g in range(g_):
            st = _dot_nt(kc_ref[g], jnp.concatenate(stack(qn_ref, g), axis=0))
            st = jnp.where(valid, st, NEG)
            e = jnp.exp2(st - jnp.maximum(jnp.max(st, axis=0, keepdims=True), 0.1 * NEG))
            den = jnp.sum(e, axis=0, keepdims=True)
            pt = e * (1.0 / jnp.where(den > 0, den, 1.0))
            o_all = _dot(pt.astype(BF16).T, vc_ref[g])
            for blk in range(PRE_BLOCKS):
                oc_ref[blk, g] = jnp.concatenate(
                    [o_all[hh * tp + blk * tq:hh * tp + (blk + 1) * tq] for hh in range(HPG)], axis=0)
            psum = pt[:, 0:tp]
            for i in range(1, HPG):
                psum = psum + pt[:, i * tp:(i + 1) * tp]
            psums.append(psum)

        psum = jnp.concatenate(psums, axis=1)
        imp = sum(_dot(ov_ref[...], part) for part in _split3(psum))[0:ns, :]
        n_i = lax.broadcasted_iota(jnp.int32, imp.shape, 0)
        tt = s0 + (lax.broadcasted_iota(jnp.int32, imp.shape, 1) & (tp - 1))
        cur = tt // SEL_BLOCK
        sel = (n_i == 0) | (n_i == cur) | (n_i == cur - 1)
        score = jnp.where(sel | (n_i * SEL_BLOCK > tt), -jnp.inf, imp)
        for _ in range(ktop - 3):
            mx = jnp.max(score, axis=0, keepdims=True)
            idx = jnp.min(jnp.where(score == mx, n_i, ns), axis=0, keepdims=True)
            hit = n_i == idx
            sel = sel | hit
            score = jnp.where(hit, -jnp.inf, score)
        bias_t = jnp.where(sel, 0.0, SEL_BIAS).astype(BF16)
        bias_t = jnp.concatenate([bias_t, jnp.zeros((SLOT - ns, g_ * tp), BF16)], axis=0)
        for g in range(g_):
            for blk in range(PRE_BLOCKS):
                c0 = g * tp + blk * tq
                bias_ref[blk, g] = bias_t[:, c0:c0 + tq].T

    pre = qi % PRE_BLOCKS
    qa = []
    for g in range(g_):
        bias = bias_ref[pre, g]
        qa.append(jnp.concatenate([q + bias for q in stack(qr_ref, g)], axis=0))

    nchunk = (qi + 2) // 2
    dif = lax.broadcasted_iota(jnp.int32, (rows, kch), 1) - r_row

    def run_chunks(n, fn):
        def body(q, carry):
            fn(UNROLL * q, UNROLL)
            return carry

        lax.fori_loop(0, n // UNROLL, body, 0)
        for rem in range(1, UNROLL):
            @pl.when(n % UNROLL == rem)
            def _(rem=rem):
                fn((n // UNROLL) * UNROLL, rem)

    def score_chunks(j, nk):
        for g in range(g_):
            mg = mx_ref[g]
            for i in range(nk):
                k0 = pl.multiple_of((j + i) * kch, kch)
                sc = _dot_nt(qa[g], ks_ref[pl.ds(k0, kch), g * SLOT:(g + 1) * SLOT])
                sc = jnp.where(dif <= s0 - k0, sc, NEG)
                s_ref[g, j + i] = sc
                mg = jnp.maximum(mg, jnp.maximum(sc[:, :SLOT], sc[:, SLOT:]))
            mx_ref[g] = mg

    mx_ref[...] = jnp.full(mx_ref.shape, NEG, F32)
    run_chunks(nchunk, score_chunks)
    for g in range(g_):
        mx_ref[g] = jnp.broadcast_to(jnp.max(mx_ref[g], axis=1, keepdims=True), (rows, SLOT))
    acc_ref[...] = jnp.zeros(acc_ref.shape, F32)

    def pv_chunks(j, nk):
        k0 = pl.multiple_of(j * kch, kch)
        ones_k = jnp.ones((nk * kch, SLOT), BF16)
        for g in range(g_):
            m = mx_ref[g]
            parts = []
            for i in range(nk):
                sc = s_ref[g, j + i]
                parts += [jnp.exp2(sc[:, :SLOT] - m), jnp.exp2(sc[:, SLOT:] - m)]
            p = jnp.concatenate(parts, axis=1).astype(BF16)
            v = jnp.concatenate([vs_ref[pl.ds(k0, nk * kch), g * SLOT:(g + 1) * SLOT], ones_k], axis=1)
            acc_ref[g] += _dot(p, v)

    run_chunks(nchunk, pv_chunks)

    ones_w = jnp.ones((wlen, SLOT), BF16)
    nwt = wlen // tq
    wdif = lax.broadcasted_iota(jnp.int32, (rows, tq), 1) - r_row
    pad_bias = jnp.where(lane == 0, SEL_BIAS, 0.0).astype(BF16)
    o_ws = []
    for g in range(g_):
        qw = jnp.concatenate([q + pad_bias for q in stack(qr_ref, g)], axis=0)
        sw = _dot_nt(qw, kwp_ref[pl.ds(s0, wlen), g * SLOT:(g + 1) * SLOT])
        tiles = [sw[:, i * tq:(i + 1) * tq] for i in range(nwt)]
        tiles[0] = jnp.where(wdif > 0, tiles[0], NEG)
        tiles[-1] = jnp.where(wdif <= 0, tiles[-1], NEG)
        mw = tiles[0]
        for t in tiles[1:]:
            mw = jnp.maximum(mw, t)
        mw = jnp.max(mw, axis=1, keepdims=True)
        pw = jnp.concatenate([jnp.exp2(t - mw) for t in tiles], axis=1).astype(BF16)
        vwin = jnp.concatenate([vwp_ref[pl.ds(s0, wlen), g * SLOT:(g + 1) * SLOT], ones_w], axis=1)
        accw = _dot(pw, vwin)
        o_ws.append(accw[:, :SLOT] / accw[:, SLOT:])

    for g in range(g_):
        acc = acc_ref[g]
        o_s = acc[:, :SLOT] / acc[:, SLOT:]
        o_w = o_ws[g]
        for hp in range(HPG // 2):
            col = (g * HPG + 2 * hp) * D_V
            ev, od = slice(2 * hp * tq, (2 * hp + 1) * tq), slice((2 * hp + 1) * tq, (2 * hp + 2) * tq)
            tile = jnp.zeros((tq, SLOT), F32)
            for br, o_b in enumerate((oc_ref[pre, g], o_s, o_w)):
                pair = jnp.where(lane < D_V, o_b[ev], o_b[od])
                tile = tile + gx_ref[pre, :, br * hw + col:br * hw + col + SLOT] * pair
            o_ref[:, col:col + SLOT] = tile.astype(BF16)


def _nsa_attn(qn, qr, gates, kc, vc, ks, vs, kw, vw, *, nb, s):
    tq = Q_BLOCK
    nq = s // tq
    nc = (s - CMP_BLOCK) // CMP_STRIDE + 1
    ncp = s // CMP_STRIDE
    ns = s // SEL_BLOCK
    ktop = min(SEL_TOPK, ns)
    g_ = N_KV_GROUPS
    assert ncp % 8 == 0 and ns <= AUG and ktop >= 3 and s % (2 * tq) == 0 and 2 * D_V == SLOT
    assert nq % PRE_BLOCKS == 0
    hw = N_HEADS * D_V
    cs = np.arange(ncp) * CMP_STRIDE
    ss = np.arange(SLOT) * SEL_BLOCK
    ov = ((cs[None, :] < ss[:, None] + SEL_BLOCK) & (cs[None, :] + CMP_BLOCK > ss[:, None])
          & (np.arange(ncp)[None, :] < nc) & (np.arange(SLOT)[:, None] < ns))
    ex = np.zeros((LANES, N_BRANCH * hw), np.float32)
    for hd in range(N_HEADS):
        for br in range(N_BRANCH):
            ex[hd * N_BRANCH + br, br * hw + hd * D_V:br * hw + (hd + 1) * D_V] = 1.0
    qspec = lambda wd: pl.BlockSpec((tq, wd), lambda b, q: (b * nq + q, 0))
    pspec = lambda wd: pl.BlockSpec((PRE_BLOCKS * tq, wd), lambda b, q: ((b * nq + q) // PRE_BLOCKS, 0))
    seq = lambda wd: pl.BlockSpec((s, wd), lambda b, q: (b, 0))
    cmp_spec = pl.BlockSpec((None, g_, ncp, SLOT), lambda b, q: (b, 0, 0, 0))
    nkc = s // (2 * tq)
    return pl.pallas_call(
        functools.partial(_nsa_kernel, nc=nc, ns=ns, ktop=ktop),
        grid=(nb, nq),
        in_specs=[pspec(N_HEADS * SLOT), qspec(N_HEADS * SLOT), pspec(LANES), cmp_spec, cmp_spec,
                  seq(g_ * SLOT), seq(g_ * SLOT), seq(g_ * SLOT), seq(g_ * SLOT),
                  _const_spec(ov.shape), _const_spec(ex.shape)],
        out_specs=qspec(hw),
        out_shape=jax.ShapeDtypeStruct((nb * s, hw), BF16),
        scratch_shapes=[pltpu.VMEM((g_, nkc, HPG * tq, 2 * tq), F32),
                        pltpu.VMEM((g_, HPG * tq, SLOT), F32),
                        pltpu.VMEM((g_, HPG * tq, 2 * SLOT), F32),
                        pltpu.VMEM((s + WINDOW, g_ * SLOT), BF16),
                        pltpu.VMEM((s + WINDOW, g_ * SLOT), BF16),
                        pltpu.VMEM((PRE_BLOCKS, g_, HPG * tq, SLOT), F32),
                        pltpu.VMEM((PRE_BLOCKS, g_, tq, SLOT), BF16),
                        pltpu.VMEM((PRE_BLOCKS, tq, N_BRANCH * hw), F32)],
        compiler_params=_cparams(("parallel", "arbitrary")), name="nsa_attn",
    )(qn, qr, gates, kc.reshape(nb, g_, ncp, SLOT), vc.reshape(nb, g_, ncp, SLOT), ks, vs, kw, vw,
      jnp.asarray(ov, BF16), jnp.asarray(ex, BF16))


def kernel(x, norm_g, ffn_w_gu, ffn_w_down, a_w_in, a_conv_w, a_conv_b, a_w_ra, a_b_ra, a_w_ix, a_b_ix,
           a_lambda, a_w_out, kv_norm_g, kv_w, cmp_pos_k, cmp_w1_k, cmp_w2_k, cmp_pos_v, cmp_w1_v,
           cmp_w2_v, b_w_q, b_gate_bias, b_w_o, final_g):
    nb, s, d = x.shape
    n_a, n_b = a_w_in.shape[0], b_w_q.shape[0]
    depth = n_a + n_b
    assert n_a >= 1 and norm_g.shape[0] == depth

    h = x
    for layer in range(n_a):
        hm = _rglru_mixer(h, norm_g[layer, 0], a_w_in[layer], a_conv_w[layer], a_conv_b[layer],
                          a_w_ra[layer], a_b_ra[layer], a_w_ix[layer], a_b_ix[layer], a_lambda[layer],
                          a_w_out[layer])
        h = _ffn(hm.reshape(nb * s, d), norm_g[layer, 1], ffn_w_gu[layer], ffn_w_down[layer],
                 final_g=final_g if layer == depth - 1 else None).reshape(nb, s, d)
    if n_b == 0:
        return h
    h = h.reshape(nb * s, d)

    tables = _rope_tables(s)
    xk, xv, ks, vs, kw, vw = _kv_proj(h, kv_norm_g, kv_w, tables, nb=nb, s=s)
    nchunk = s // CMP_STRIDE
    w2k = jnp.pad(cmp_w2_k, ((0, 0), (AUG, 0)))
    w2v = jnp.tile(cmp_w2_v, (1, SLOT // D_V))
    kc = _compress(xk.reshape(-1, xk.shape[-1]), cmp_pos_k, cmp_w1_k, w2k, nchunk=nchunk)
    vc = _compress(xv.reshape(-1, xv.shape[-1]), cmp_pos_v, cmp_w1_v, w2v, nchunk=nchunk)

    for j in range(n_b):
        layer = n_a + j
        qn, qr, gates = _q_proj(h, norm_g[layer, 0], b_w_q[j], b_gate_bias[j], tables, s=s)
        attn = _nsa_attn(qn, qr, gates, kc, vc, ks, vs, kw, vw, nb=nb, s=s)
        h = _ffn(h, norm_g[layer, 1], ffn_w_gu[layer], ffn_w_down[layer], attn=attn, w_o=b_w_o[j],
                 final_g=final_g if layer == depth - 1 else None)
    return h.reshape(nb, s, d)
```

```python
import functools
import math

import numpy as np
import jax
import jax.numpy as jnp
from jax import lax
from jax.experimental import pallas as pl
from jax.experimental.pallas import tpu as pltpu

CONV_WIDTH = 4
RGLRU_C = 8.0
N_HEADS = 16
N_KV_GROUPS = 4
HPG = N_HEADS // N_KV_GROUPS
D_QK = 96
D_V = 64
ROPE_DIMS = D_QK // 4
ROPE_HALF = ROPE_DIMS // 2
ROPE_THETA = 500000.0
CMP_BLOCK = 32
CMP_STRIDE = 16
SEL_BLOCK = 64
SEL_TOPK = 8
WINDOW = 512
Q_BLOCK = 128
N_BRANCH = 3
EPS = 1e-6

LANES = 128
MXU_TILE = 256
SLOT = 128
AUG = SLOT - D_QK
VMEM_LIMIT = 56 * 1024 * 1024
NEG = -1e30
SEL_BIAS = -1e9
LOG2E = math.log2(math.e)
UNROLL = 4
PRE_BLOCKS = 2

F32 = jnp.float32
BF16 = jnp.bfloat16


def _cparams(sem):
    return pltpu.CompilerParams(dimension_semantics=sem, vmem_limit_bytes=VMEM_LIMIT)


def _const_spec(shape):
    nd = len(shape)
    return pl.BlockSpec(shape, lambda *_: (0,) * nd, pipeline_mode=pl.Buffered(1))


def _rms(x, g):
    return x * lax.rsqrt(jnp.mean(x * x, axis=-1, keepdims=True) + EPS) * g


def _gelu(x):
    k1 = -2.0 * math.sqrt(2.0 / math.pi) * LOG2E
    return x / (1.0 + jnp.exp2(x * (k1 + (k1 * 0.044715) * (x * x))))


def _dot(a, b):
    return jnp.dot(a, b, preferred_element_type=F32)


def _dot_nt(a, b):
    return lax.dot_general(a, b, (((1,), (1,)), ((), ())), preferred_element_type=F32)


def _rglru_kernel(x_hbm, g_ref, win_ref, cw_ref, cb_ref, wg_ref, bra_ref, bix_ref, lam_ref, wout_ref,
                  o_hbm, xbe_ref, h_ref, ry_ref, xt_ref, ot_ref, sem_in, sem_out, *, ts, nb, cw, nch, nsteps):
    rows = ts * nb
    halo = (CONV_WIDTH - 1) * nb
    d = xt_ref.shape[-1]
    i = pl.program_id(0)
    slot = i % 2

    def in_copies(step, sl):
        return [pltpu.make_async_copy(x_hbm.at[:, step * ts + t, :], xt_ref.at[sl, t], sem_in.at[sl])
                for t in range(ts)]

    def out_copies(step, sl):
        return [pltpu.make_async_copy(ot_ref.at[sl, t], o_hbm.at[:, step * ts + t, :], sem_out.at[sl])
                for t in range(ts)]

    @pl.when(i == 0)
    def _():
        xbe_ref[:, 0:halo, :] = jnp.zeros((nch, halo, cw), F32)
        h_ref[...] = jnp.zeros_like(h_ref)
        for cp in in_copies(0, 0):
            cp.start()

    @pl.when(i + 1 < nsteps)
    def _():
        for cp in in_copies(i + 1, 1 - slot):
            cp.start()

    for cp in in_copies(i, slot):
        cp.wait()

    hn = _rms(xt_ref[slot].reshape(rows, d), g_ref[...]).astype(BF16)
    for c in range(nch):
        u = _dot(hn, win_ref[c])
        xbe_ref[c, halo:halo + rows, :] = u[:, :cw]
        y = _gelu(u[:, cw:])
        xc = cb_ref[c]
        for j in range(CONV_WIDTH):
            xc = xc + cw_ref[c, j:j + 1, :] * xbe_ref[c, j * nb:j * nb + rows, :]
        xbe_ref[c, 0:halo, :] = xbe_ref[c, rows:rows + halo, :]
        gt = _dot(xc.astype(BF16), wg_ref[c])
        r = jax.nn.sigmoid(gt[:, :cw] + bra_ref[c])
        ig = jax.nn.sigmoid(gt[:, cw:] + bix_ref[c])
        nla = r * (RGLRU_C * jax.nn.softplus(-lam_ref[c]))
        a = jnp.exp2(nla * (-LOG2E))
        om = jnp.tanh(nla) * (1.0 + a * a)
        bx = jnp.where(om > 0, om * lax.rsqrt(om), 0.0) * (ig * xc)
        h = h_ref[c]
        for t in range(ts):
            sl = slice(t * nb, (t + 1) * nb)
            h = a[sl] * h + bx[sl]
            ry_ref[sl, c * cw:(c + 1) * cw] = (h * y[sl]).astype(BF16)
        h_ref[c] = h
    out = _dot(ry_ref[...], wout_ref[...]) + xt_ref[slot].reshape(rows, d)

    @pl.when(i >= 2)
    def _():
        for cp in out_copies(i - 2, slot):
            cp.wait()

    ot_ref[slot] = out.reshape(ts, nb, d)
    for cp in out_copies(i, slot):
        cp.start()

    @pl.when(i == nsteps - 1)
    def _():
        for cp in out_copies(i, slot):
            cp.wait()
        if nsteps > 1:
            for cp in out_copies(i - 1, 1 - slot):
                cp.wait()


def _rglru_mixer(x, g, w_in, conv_w, conv_b, w_ra, b_ra, w_ix, b_ix, lam, w_out):
    nb, s, _ = x.shape
    d = w_in.shape[0]
    r = w_in.shape[1] // 2
    nblk, rb = w_ra.shape[0], w_ra.shape[1]
    per = LANES // math.gcd(rb, LANES)
    cw = per * rb
    nch = nblk // per
    assert nch * per == nblk and nch * cw == r
    ts = 32
    assert s % ts == 0 and nb % 16 == 0
    rows = ts * nb

    def chunks(v):
        return v.reshape(v.shape[0], nch, cw).transpose(1, 0, 2)

    def blockdiag(w):
        w = w.reshape(nch, per, rb, rb)
        return jnp.einsum('cpij,pq->cpiqj', w, jnp.eye(per, dtype=w.dtype)).reshape(nch, cw, cw)

    win = jnp.concatenate([chunks(w_in[:, :r]), chunks(w_in[:, r:])], axis=-1).astype(BF16)
    wg = jnp.concatenate([blockdiag(w_ra), blockdiag(w_ix)], axis=-1).astype(BF16)
    args = (x, g.reshape(1, d), win, chunks(conv_w), chunks(conv_b.reshape(1, r)), wg,
            chunks(b_ra.reshape(1, r)), chunks(b_ix.reshape(1, r)), chunks(lam.reshape(1, r)),
            w_out.astype(BF16))
    hbm = pl.BlockSpec(memory_space=pl.ANY)
    in_specs = [hbm] + [_const_spec(a.shape) for a in args[1:]]
    kern = functools.partial(_rglru_kernel, ts=ts, nb=nb, cw=cw, nch=nch, nsteps=s // ts)
    return pl.pallas_call(
        kern,
        grid=(s // ts,),
        in_specs=in_specs,
        out_specs=hbm,
        out_shape=jax.ShapeDtypeStruct((nb, s, d), F32),
        scratch_shapes=[
            pltpu.VMEM((nch, rows + (CONV_WIDTH - 1) * nb, cw), F32),
            pltpu.VMEM((nch, nb, cw), F32),
            pltpu.VMEM((rows, r), BF16),
            pltpu.VMEM((2, ts, nb, d), F32),
            pltpu.VMEM((2, ts, nb, d), F32),
            pltpu.SemaphoreType.DMA((2,)),
            pltpu.SemaphoreType.DMA((2,)),
        ],
        compiler_params=_cparams(("arbitrary",)),
        name="rglru_mixer",
    )(*args)


def _ffn_kernel(*refs, f, bounds, pre_attn, final_norm):
    it = iter(refs)
    h_ref = next(it)
    if pre_attn:
        attn_ref, wo_ref = next(it), next(it)
    g_ref, wgu_ref, wdn_ref = next(it), next(it), next(it)
    if final_norm:
        gf_ref = next(it)
    o_ref = next(it)

    h = h_ref[...]
    if pre_attn:
        h = h + _dot(attn_ref[...], wo_ref[...])
    hn = _rms(h, g_ref[...]).astype(BF16)
    acc = h
    for lo, hi in zip(bounds[:-1], bounds[1:]):
        gate = _dot(hn, wgu_ref[:, lo:hi])
        val = _dot(hn, wgu_ref[:, f + lo:f + hi])
        act = (jax.nn.silu(gate) * val).astype(BF16)
        acc = acc + _dot(act, wdn_ref[lo:hi, :])
    if final_norm:
        acc = _rms(acc, gf_ref[...])
    o_ref[...] = acc


def _ffn(h, g, w_gu, w_down, *, attn=None, w_o=None, final_g=None):
    n, d = h.shape
    f = w_down.shape[0]
    tr = 1024
    step = 3 * MXU_TILE
    bounds = tuple(range(0, f, step)) + (f,)
    assert n % tr == 0 and f % LANES == 0
    pre_attn = attn is not None
    final_norm = final_g is not None
    row_spec = lambda w: pl.BlockSpec((tr, w), lambda i: (i, 0))
    args, specs = [h], [row_spec(d)]
    if pre_attn:
        args += [attn, w_o.astype(BF16)]
        specs += [row_spec(attn.shape[1]), _const_spec(w_o.shape)]
    args += [g.reshape(1, d), w_gu.astype(BF16), w_down.astype(BF16)]
    specs += [_const_spec((1, d)), _const_spec(w_gu.shape), _const_spec(w_down.shape)]
    if final_norm:
        args.append(final_g.reshape(1, d))
        specs.append(_const_spec((1, d)))
    kern = functools.partial(_ffn_kernel, f=f, bounds=bounds, pre_attn=pre_attn, final_norm=final_norm)
    return pl.pallas_call(
        kern, grid=(n // tr,), in_specs=specs, out_specs=row_spec(d),
        out_shape=jax.ShapeDtypeStruct((n, d), F32),
        compiler_params=_cparams(("parallel",)), name="ffn",
    )(*args)


def _rope_slot(x, cos, sin_lo, sin_hi):
    return (x * cos + pltpu.roll(x, SLOT - ROPE_HALF, 1) * sin_lo
            + pltpu.roll(x, ROPE_HALF, 1) * sin_hi)


def _kv_proj_kernel(h_ref, g_ref, w_ref, cos_ref, slo_ref, shi_ref,
                    xk_ref, xv_ref, ks_ref, vs_ref, kw_ref, vw_ref, sc_ref, *, tr, s):
    g_ = N_KV_GROUPS
    hn = _rms(h_ref[...], g_ref[...]).astype(BF16)
    u = _dot(hn, w_ref[...])
    slot = lambda i: u[:, i * SLOT:(i + 1) * SLOT]
    cos, slo, shi = cos_ref[...], slo_ref[...], shi_ref[...]
    pos0 = (pl.program_id(0) % (s // tr)) * tr
    pos = pos0 + lax.broadcasted_iota(jnp.int32, (tr, SLOT), 0)
    lane = lax.broadcasted_iota(jnp.int32, (tr, SLOT), 1)
    onehot = (lane == pos // SEL_BLOCK).astype(F32)

    nj = tr // CMP_STRIDE
    for i in range(2 * g_):
        sc_ref[i] = slot(i)
    for x_ref, base in ((xk_ref, 0), (xv_ref, g_)):
        for gi in range(g_):
            for l in range(CMP_STRIDE):
                x_ref[gi, :, l * SLOT:(l + 1) * SLOT] = sc_ref[base + gi, pl.ds(l, nj, stride=CMP_STRIDE), :]

    vbase = 4 * g_
    for k_ref, v_ref, base, aug in ((ks_ref, vs_ref, 2 * g_, True), (kw_ref, vw_ref, 3 * g_, False)):
        for gi in range(g_):
            k = _rope_slot(slot(base + gi), cos, slo, shi)
            if aug:
                k = k + onehot
            k_ref[:, gi * SLOT:(gi + 1) * SLOT] = k.astype(BF16)
        for pair in range(g_ // 2):
            x = slot(vbase + pair)
            y = pltpu.roll(x, D_V, 1)
            v_ref[:, 2 * pair * SLOT:(2 * pair + 1) * SLOT] = jnp.where(lane < D_V, x, y).astype(BF16)
            v_ref[:, (2 * pair + 1) * SLOT:(2 * pair + 2) * SLOT] = jnp.where(lane < D_V, y, x).astype(BF16)
        vbase += g_ // 2


def _q_proj_kernel(h_ref, g_ref, w_ref, gb_ref, cos_ref, slo_ref, shi_ref, qn_ref, qr_ref, gt_ref):
    hn = _rms(h_ref[...], g_ref[...]).astype(BF16)
    u = _dot(hn, w_ref[...])
    cos, slo, shi = cos_ref[...], slo_ref[...], shi_ref[...]
    scale = D_QK ** -0.5 * LOG2E
    for hd in range(N_HEADS):
        q = u[:, hd * SLOT:(hd + 1) * SLOT] * scale
        qn_ref[:, hd * SLOT:(hd + 1) * SLOT] = q.astype(BF16)
        qr_ref[:, hd * SLOT:(hd + 1) * SLOT] = _rope_slot(q, cos, slo, shi).astype(BF16)
    gt_ref[...] = jax.nn.sigmoid(u[:, N_HEADS * SLOT:] + gb_ref[...])


def _pad_slots(w, n, width, left):
    d = w.shape[0]
    w = w.reshape(d, n, width)
    return jnp.pad(w, ((0, 0), (0, 0), (left, SLOT - width - left))).reshape(d, n * SLOT)


def _rope_tables(s):
    inv = ROPE_THETA ** (-jnp.arange(ROPE_HALF, dtype=F32) * 2.0 / ROPE_DIMS)
    ang = jnp.arange(s, dtype=F32)[:, None] * inv[None, :]
    cos, sin = jnp.cos(ang), jnp.sin(ang)
    z = lambda n: jnp.zeros((s, n), F32)
    o = lambda n: jnp.ones((s, n), F32)
    rest = SLOT - AUG - ROPE_DIMS
    cos_t = jnp.concatenate([o(AUG), cos, cos, o(rest)], axis=1)
    sin_lo = jnp.concatenate([z(AUG), -sin, z(ROPE_HALF), z(rest)], axis=1)
    sin_hi = jnp.concatenate([z(AUG), z(ROPE_HALF), sin, z(rest)], axis=1)
    return cos_t, sin_lo, sin_hi


def _kv_proj(h, g, kv_w, tables, *, nb, s):
    n, d = h.shape
    g_, dk, dv = N_KV_GROUPS, D_QK, D_V
    tr = 1024
    assert n % tr == 0 and s % tr == 0 and tr % CMP_STRIDE == 0
    sizes = [g_ * dk, g_ * dv] * 3
    offs = np.concatenate([[0], np.cumsum(sizes)])
    kc, vc, ksl, vsl, kwn, vwn = [kv_w[:, offs[j]:offs[j + 1]] for j in range(6)]
    assert 2 * dv == SLOT and g_ % 2 == 0
    w = jnp.concatenate([_pad_slots(kc, g_, dk, 0), _pad_slots(vc, g_, dv, 0),
                         _pad_slots(ksl, g_, dk, AUG), _pad_slots(kwn, g_, dk, AUG), vsl, vwn],
                        axis=1).astype(BF16)
    row = lambda wd, dt: (pl.BlockSpec((tr, wd), lambda i: (i, 0)), jax.ShapeDtypeStruct((n, wd), dt))
    nj = tr // CMP_STRIDE
    spt = s // tr
    xcmp = (pl.BlockSpec((None, g_, nj, CMP_STRIDE * SLOT), lambda i: (i // spt, 0, i % spt, 0)),
            jax.ShapeDtypeStruct((nb, g_, s // CMP_STRIDE, CMP_STRIDE * SLOT), F32))
    outs = [xcmp, xcmp, row(g_ * SLOT, BF16), row(g_ * SLOT, BF16), row(g_ * SLOT, BF16), row(g_ * SLOT, BF16)]
    tab_spec = pl.BlockSpec((tr, SLOT), lambda i: (i % spt, 0))
    return pl.pallas_call(
        functools.partial(_kv_proj_kernel, tr=tr, s=s),
        grid=(n // tr,),
        in_specs=[pl.BlockSpec((tr, d), lambda i: (i, 0)), _const_spec((1, d)), _const_spec(w.shape),
                  tab_spec, tab_spec, tab_spec],
        out_specs=[o[0] for o in outs], out_shape=[o[1] for o in outs],
        scratch_shapes=[pltpu.VMEM((2 * g_, tr, SLOT), F32)],
        compiler_params=_cparams(("parallel",)), name="kv_proj",
    )(h, g.reshape(1, d), w, *tables)


def _q_proj(h, g, w_q, gate_bias, tables, *, s):
    n, d = h.shape
    tr = 1024
    nq = N_HEADS * D_QK
    ngate = N_HEADS * N_BRANCH
    assert ngate <= LANES
    w = jnp.concatenate([_pad_slots(w_q[:, :nq], N_HEADS, D_QK, AUG),
                         jnp.pad(w_q[:, nq:], ((0, 0), (0, LANES - ngate)))], axis=1).astype(BF16)
    gb = jnp.pad(gate_bias.reshape(1, ngate), ((0, 0), (0, LANES - ngate)))
    tab_spec = pl.BlockSpec((tr, SLOT), lambda i: (i % (s // tr), 0))
    row = lambda wd, dt: (pl.BlockSpec((tr, wd), lambda i: (i, 0)), jax.ShapeDtypeStruct((n, wd), dt))
    outs = [row(N_HEADS * SLOT, BF16), row(N_HEADS * SLOT, BF16), row(LANES, F32)]
    return pl.pallas_call(
        _q_proj_kernel,
        grid=(n // tr,),
        in_specs=[pl.BlockSpec((tr, d), lambda i: (i, 0)), _const_spec((1, d)), _const_spec(w.shape),
                  _const_spec((1, LANES)), tab_spec, tab_spec, tab_spec],
        out_specs=[o[0] for o in outs], out_shape=[o[1] for o in outs],
        compiler_params=_cparams(("parallel",)), name="q_proj",
    )(h, g.reshape(1, d), w, gb, *tables)


def _compress_kernel(x_ref, plo_ref, phi_ref, w1lo_ref, w1hi_ref, w2_ref, o_ref):
    x = x_ref[...]
    lo = _dot((x + plo_ref[...]).astype(BF16), w1lo_ref[...])
    hi = _dot((x + phi_ref[...]).astype(BF16), w1hi_ref[...])
    pre = lo + pltpu.roll(hi, hi.shape[0] - 1, 0)
    o_ref[...] = _dot(_gelu(pre).astype(BF16), w2_ref[...]).astype(BF16)


def _compress(x, pos, w1, w2_slot, *, nchunk):
    n, width = x.shape
    dh = pos.shape[1]
    hid = w1.shape[1]
    assert CMP_BLOCK == 2 * CMP_STRIDE and width == CMP_STRIDE * SLOT
    tr = 4 * nchunk
    assert n % tr == 0

    def slots(v):
        k = v.shape[1]
        v = jnp.pad(v.reshape(2, CMP_STRIDE, dh, k), ((0, 0), (0, 0), (0, SLOT - dh), (0, 0)))
        return v.reshape(2, width, k)

    p = slots(pos.reshape(CMP_BLOCK * dh, 1))[:, :, 0].reshape(2, 1, width)
    w1s = slots(w1).astype(BF16)
    return pl.pallas_call(
        _compress_kernel,
        grid=(n // tr,),
        in_specs=[pl.BlockSpec((tr, width), lambda i: (i, 0)), _const_spec((1, width)), _const_spec((1, width)),
                  _const_spec((width, hid)), _const_spec((width, hid)), _const_spec((hid, SLOT))],
        out_specs=pl.BlockSpec((tr, SLOT), lambda i: (i, 0)),
        out_shape=jax.ShapeDtypeStruct((n, SLOT), BF16),
        compiler_params=_cparams(("parallel",)), name="compress",
    )(x, p[0], p[1], w1s[0], w1s[1], w2_slot.astype(BF16))


def _split3(x):
    hi = x.astype(BF16)
    r1 = x - hi.astype(F32)
    mid = r1.astype(BF16)
    lo = (r1 - mid.astype(F32)).astype(BF16)
    return hi, mid, lo


def _nsa_kernel(qn_ref, qr_ref, gt_ref, kc_ref, vc_ref, ks_ref, vs_ref, kw_ref, vw_ref, ov_ref, ex_ref,
                o_ref, s_ref, mx_ref, acc_ref, kwp_ref, vwp_ref, oc_ref, bias_ref, gx_ref, *, nc, ns, ktop):
    g_ = N_KV_GROUPS
    tq = Q_BLOCK
    rows = HPG * tq
    kch = 2 * tq
    wlen = WINDOW + tq
    qi = pl.program_id(1)
    s0 = pl.multiple_of(qi * tq, tq)
    hw = N_HEADS * D_V
    lane = lax.broadcasted_iota(jnp.int32, (tq, SLOT), 1)

    @pl.when(qi == 0)
    def _():
        pad_lane = lax.broadcasted_iota(jnp.int32, (WINDOW, g_ * SLOT), 1) & (SLOT - 1)
        kwp_ref[0:WINDOW, :] = (pad_lane == 0).astype(BF16)
        vwp_ref[0:WINDOW, :] = jnp.zeros((WINDOW, g_ * SLOT), BF16)
        kwp_ref[WINDOW:, :] = kw_ref[...]
        vwp_ref[WINDOW:, :] = vw_ref[...]

    r_row = lax.broadcasted_iota(jnp.int32, (rows, 1), 0) & (tq - 1)
    stack = lambda ref, g: [ref[:, hd * SLOT:(hd + 1) * SLOT] for hd in range(g * HPG, (g + 1) * HPG)]

    @pl.when(qi % PRE_BLOCKS == 0)
    def _():
        tp = PRE_BLOCKS * tq
        ghi, gmid, _ = _split3(gt_ref[...])
        gx_all = _dot(ghi, ex_ref[...]) + _dot(gmid, ex_ref[...])
        for blk in range(PRE_BLOCKS):
            gx_ref[blk] = gx_all[blk * tq:(blk + 1) * tq]
        ncp = kc_ref.shape[1]
        c_idx = lax.broadcasted_iota(jnp.int32, (ncp, 1), 0)
        c_end = jnp.where(c_idx < nc, c_idx * CMP_STRIDE + (CMP_BLOCK - 1), jnp.iinfo(jnp.int32).max)
        t_col = s0 + (lax.broadcasted_iota(jnp.int32, (1, HPG * tp), 1) & (tp - 1))
        valid = c_end <= t_col
        psums = []
        for g in range(g_):
            st = _dot_nt(kc_ref[g], jnp.concatenate(stack(qn_ref, g), axis=0))
            st = jnp.where(valid, st, NEG)
            e = jnp.exp2(st - jnp.maximum(jnp.max(st, axis=0, keepdims=True), 0.1 * NEG))
            den = jnp.sum(e, axis=0, keepdims=True)
            pt = e * (1.0 / jnp.where(den > 0, den, 1.0))
            o_all = _dot(pt.astype(BF16).T, vc_ref[g])
            for blk in range(PRE_BLOCKS):
                oc_ref[blk, g] = jnp.concatenate(
                    [o_all[hh * tp + blk * tq:hh * tp + (blk + 1) * tq] for hh in range(HPG)], axis=0)
            psum = pt[:, 0:tp]
            for i in range(1, HPG):
                psum = psum + pt[:, i * tp:(i + 1) * tp]
            psums.append(psum)

        psum = jnp.concatenate(psums, axis=1)
        imp = sum(_dot(ov_ref[...], part) for part in _split3(psum))[0:ns, :]
        n_i = lax.broadcasted_iota(jnp.int32, imp.shape, 0)
        tt = s0 + (lax.broadcasted_iota(jnp.int32, imp.shape, 1) & (tp - 1))
        cur = tt // SEL_BLOCK
        sel = (n_i == 0) | (n_i == cur) | (n_i == cur - 1)
        score = jnp.where(sel | (n_i * SEL_BLOCK > tt), -jnp.inf, imp)
        for _ in range(ktop - 3):
            mx = jnp.max(score, axis=0, keepdims=True)
            idx = jnp.min(jnp.where(score == mx, n_i, ns), axis=0, keepdims=True)
            hit = n_i == idx
            sel = sel | hit
            score = jnp.where(hit, -jnp.inf, score)
        bias_t = jnp.where(sel, 0.0, SEL_BIAS).astype(BF16)
        bias_t = jnp.concatenate([bias_t, jnp.zeros((SLOT - ns, g_ * tp), BF16)], axis=0)
        for g in range(g_):
            for blk in range(PRE_BLOCKS):
                c0 = g * tp + blk * tq
                bias_ref[blk, g] = bias_t[:, c0:c0 + tq].T

    pre = qi % PRE_BLOCKS
    qa = []
    for g in range(g_):
        bias = bias_ref[pre, g]
        qa.append(jnp.concatenate([q + bias for q in stack(qr_ref, g)], axis=0))

    nchunk = (qi + 2) // 2
    dif = lax.broadcasted_iota(jnp.int32, (rows, kch), 1) - r_row

    def run_chunks(n, fn):
        def body(q, carry):
            fn(UNROLL * q, UNROLL)
            return carry

        lax.fori_loop(0, n // UNROLL, body, 0)
        for rem in range(1, UNROLL):
            @pl.when(n % UNROLL == rem)
            def _(rem=rem):
                fn((n // UNROLL) * UNROLL, rem)

    def score_chunks(j, nk):
        for g in range(g_):
            mg = mx_ref[g]
            for i in range(nk):
                k0 = pl.multiple_of((j + i) * kch, kch)
                sc = _dot_nt(qa[g], ks_ref[pl.ds(k0, kch), g * SLOT:(g + 1) * SLOT])
                sc = jnp.where(dif <= s0 - k0, sc, NEG)
                s_ref[g, j + i] = sc
                mg = jnp.maximum(mg, jnp.maximum(sc[:, :SLOT], sc[:, SLOT:]))
            mx_ref[g] = mg

    mx_ref[...] = jnp.full(mx_ref.shape, NEG, F32)
    run_chunks(nchunk, score_chunks)
    for g in range(g_):
        mx_ref[g] = jnp.broadcast_to(jnp.max(mx_ref[g], axis=1, keepdims=True), (rows, SLOT))
    acc_ref[...] = jnp.zeros(acc_ref.shape, F32)

    def pv_chunks(j, nk):
        k0 = pl.multiple_of(j * kch, kch)
        ones_k = jnp.ones((nk * kch, SLOT), BF16)
        for g in range(g_):
            m = mx_ref[g]
            parts = []
            for i in range(nk):
                sc = s_ref[g, j + i]
                parts += [jnp.exp2(sc[:, :SLOT] - m), jnp.exp2(sc[:, SLOT:] - m)]
            p = jnp.concatenate(parts, axis=1).astype(BF16)
            v = jnp.concatenate([vs_ref[pl.ds(k0, nk * kch), g * SLOT:(g + 1) * SLOT], ones_k], axis=1)
            acc_ref[g] += _dot(p, v)

    run_chunks(nchunk, pv_chunks)

    ones_w = jnp.ones((wlen, SLOT), BF16)
    nwt = wlen // tq
    wdif = lax.broadcasted_iota(jnp.int32, (rows, tq), 1) - r_row
    pad_bias = jnp.where(lane == 0, SEL_BIAS, 0.0).astype(BF16)
    o_ws = []
    for g in range(g_):
        qw = jnp.concatenate([q + pad_bias for q in stack(qr_ref, g)], axis=0)
        sw = _dot_nt(qw, kwp_ref[pl.ds(s0, wlen), g * SLOT:(g + 1) * SLOT])
        tiles = [sw[:, i * tq:(i + 1) * tq] for i in range(nwt)]
        tiles[0] = jnp.where(wdif > 0, tiles[0], NEG)
        tiles[-1] = jnp.where(wdif <= 0, tiles[-1], NEG)
        mw = tiles[0]
        for t in tiles[1:]:
            mw = jnp.maximum(mw, t)
        mw = jnp.max(mw, axis=1, keepdims=True)
        pw = jnp.concatenate([jnp.exp2(t - mw) for t in tiles], axis=1).astype(BF16)
        vwin = jnp.concatenate([vwp_ref[pl.ds(s0, wlen), g * SLOT:(g + 1) * SLOT], ones_w], axis=1)
        accw = _dot(pw, vwin)
        o_ws.append(accw[:, :SLOT] / accw[:, SLOT:])

    for g in range(g_):
        acc = acc_ref[g]
        o_s = acc[:, :SLOT] / acc[:, SLOT:]
        o_w = o_ws[g]
        for hp in range(HPG // 2):
            col = (g * HPG + 2 * hp) * D_V
            ev, od = slice(2 * hp * tq, (2 * hp + 1) * tq), slice((2 * hp + 1) * tq, (2 * hp + 2) * tq)
            tile = jnp.zeros((tq, SLOT), F32)
            for br, o_b in enumerate((oc_ref[pre, g], o_s, o_w)):
                pair = jnp.where(lane < D_V, o_b[ev], o_b[od])
                tile = tile + gx_ref[pre, :, br * hw + col:br * hw + col + SLOT] * pair
            o_ref[:, col:col + SLOT] = tile.astype(BF16)


def _nsa_attn(qn, qr, gates, kc, vc, ks, vs, kw, vw, *, nb, s):
    tq = Q_BLOCK
    nq = s // tq
    nc = (s - CMP_BLOCK) // CMP_STRIDE + 1
    ncp = s // CMP_STRIDE
    ns = s // SEL_BLOCK
    ktop = min(SEL_TOPK, ns)
    g_ = N_KV_GROUPS
    assert ncp % 8 == 0 and ns <= AUG and ktop >= 3 and s % (2 * tq) == 0 and 2 * D_V == SLOT
    assert nq % PRE_BLOCKS == 0
    hw = N_HEADS * D_V
    cs = np.arange(ncp) * CMP_STRIDE
    ss = np.arange(SLOT) * SEL_BLOCK
    ov = ((cs[None, :] < ss[:, None] + SEL_BLOCK) & (cs[None, :] + CMP_BLOCK > ss[:, None])
          & (np.arange(ncp)[None, :] < nc) & (np.arange(SLOT)[:, None] < ns))
    ex = np.zeros((LANES, N_BRANCH * hw), np.float32)
    for hd in range(N_HEADS):
        for br in range(N_BRANCH):
            ex[hd * N_BRANCH + br, br * hw + hd * D_V:br * hw + (hd + 1) * D_V] = 1.0
    qspec = lambda wd: pl.BlockSpec((tq, wd), lambda b, q: (b * nq + q, 0))
    pspec = lambda wd: pl.BlockSpec((PRE_BLOCKS * tq, wd), lambda b, q: ((b * nq + q) // PRE_BLOCKS, 0))
    seq = lambda wd: pl.BlockSpec((s, wd), lambda b, q: (b, 0))
    cmp_spec = pl.BlockSpec((None, g_, ncp, SLOT), lambda b, q: (b, 0, 0, 0))
    nkc = s // (2 * tq)
    return pl.pallas_call(
        functools.partial(_nsa_kernel, nc=nc, ns=ns, ktop=ktop),
        grid=(nb, nq),
        in_specs=[pspec(N_HEADS * SLOT), qspec(N_HEADS * SLOT), pspec(LANES), cmp_spec, cmp_spec,
                  seq(g_ * SLOT), seq(g_ * SLOT), seq(g_ * SLOT), seq(g_ * SLOT),
                  _const_spec(ov.shape), _const_spec(ex.shape)],
        out_specs=qspec(hw),
        out_shape=jax.ShapeDtypeStruct((nb * s, hw), BF16),
        scratch_shapes=[pltpu.VMEM((g_, nkc, HPG * tq, 2 * tq), F32),
                        pltpu.VMEM((g_, HPG * tq, SLOT), F32),
                        pltpu.VMEM((g_, HPG * tq, 2 * SLOT), F32),
                        pltpu.VMEM((s + WINDOW, g_ * SLOT), BF16),
                        pltpu.VMEM((s + WINDOW, g_ * SLOT), BF16),
                        pltpu.VMEM((PRE_BLOCKS, g_, HPG * tq, SLOT), F32),
                        pltpu.VMEM((PRE_BLOCKS, g_, tq, SLOT), BF16),
                        pltpu.VMEM((PRE_BLOCKS, tq, N_BRANCH * hw), F32)],
        compiler_params=_cparams(("parallel", "arbitrary")), name="nsa_attn",
    )(qn, qr, gates, kc.reshape(nb, g_, ncp, SLOT), vc.reshape(nb, g_, ncp, SLOT), ks, vs, kw, vw,
      jnp.asarray(ov, BF16), jnp.asarray(ex, BF16))


def kernel(x, norm_g, ffn_w_gu, ffn_w_down, a_w_in, a_conv_w, a_conv_b, a_w_ra, a_b_ra, a_w_ix, a_b_ix,
           a_lambda, a_w_out, kv_norm_g, kv_w, cmp_pos_k, cmp_w1_k, cmp_w2_k, cmp_pos_v, cmp_w1_v,
           cmp_w2_v, b_w_q, b_gate_bias, b_w_o, final_g):
    nb, s, d = x.shape
    n_a, n_b = a_w_in.shape[0], b_w_q.shape[0]
    depth = n_a + n_b
    assert n_a >= 1 and norm_g.shape[0] == depth

    h = x
    for layer in range(n_a):
        hm = _rglru_mixer(h, norm_g[layer, 0], a_w_in[layer], a_conv_w[layer], a_conv_b[layer],
                          a_w_ra[layer], a_b_ra[layer], a_w_ix[layer], a_b_ix[layer], a_lambda[layer],
                          a_w_out[layer])
        h = _ffn(hm.reshape(nb * s, d), norm_g[layer, 1], ffn_w_gu[layer], ffn_w_down[layer],
                 final_g=final_g if layer == depth - 1 else None).reshape(nb, s, d)
    if n_b == 0:
        return h
    h = h.reshape(nb * s, d)

    tables = _rope_tables(s)
    xk, xv, ks, vs, kw, vw = _kv_proj(h, kv_norm_g, kv_w, tables, nb=nb, s=s)
    nchunk = s // CMP_STRIDE
    w2k = jnp.pad(cmp_w2_k, ((0, 0), (AUG, 0)))
    w2v = jnp.tile(cmp_w2_v, (1, SLOT // D_V))
    kc = _compress(xk.reshape(-1, xk.shape[-1]), cmp_pos_k, cmp_w1_k, w2k, nchunk=nchunk)
    vc = _compress(xv.reshape(-1, xv.shape[-1]), cmp_pos_v, cmp_w1_v, w2v, nchunk=nchunk)

    for j in range(n_b):
        layer = n_a + j
        qn, qr, gates = _q_proj(h, norm_g[layer, 0], b_w_q[j], b_gate_bias[j], tables, s=s)
        attn = _nsa_attn(qn, qr, gates, kc, vc, ks, vs, kw, vw, nb=nb, s=s)
        h = _ffn(h, norm_g[layer, 1], ffn_w_gu[layer], ffn_w_down[layer], attn=attn, w_o=b_w_o[j],
                 final_g=final_g if layer == depth - 1 else None)
    return h.reshape(nb, s, d)
```
